```python
import math
import jax, jax.numpy as jnp
from jax import lax
import numpy as np

D_MODEL = 1024
BATCH = 8
SEQ = 8192
DEPTH = 1

GLA_HEADS = 4
GLA_DK = 64
GLA_DV = 128
GLA_GATE_RANK = 16
GLA_TAU = 16.0
GLA_CHUNK = 64
DSA_HEADS = 8
DSA_HEAD_DIM = 64
IDX_HEADS = 8
IDX_DIM = 32
DSA_TOPK_MAX = 256
DSA_QBLOCK = 128
ROPE_THETA = 500000.0
ROPE_FRACTION = 4
N_BRANCH = 2
BRANCH_WIDTH = 512
D_FF = 2816
CONV_WIDTH = 3
LN_EPS = 1e-5
RMS_EPS = 1e-6
DEEPNORM_ALPHA = (2.0 * DEPTH) ** 0.25
DEEPNORM_BETA = (8.0 * DEPTH) ** -0.25

IN_WIDTHS = (
    GLA_HEADS * GLA_DK,
    GLA_HEADS * GLA_DK,
    GLA_HEADS * GLA_DV,
    GLA_GATE_RANK,
    GLA_HEADS * GLA_DV,
    DSA_HEADS * DSA_HEAD_DIM,
    DSA_HEAD_DIM,
    DSA_HEAD_DIM,
    IDX_HEADS * IDX_DIM,
    IDX_DIM,
    IDX_HEADS,
    N_BRANCH * D_MODEL,
)
D_IN = sum(IN_WIDTHS)

kernel_name = "hybrid_gla_dsa_gated_merge_deepnorm"


def split_cols(h, widths):
    outs = []
    off = 0
    for w in widths:
        outs.append(h[..., off:off + w])
        off += w
    return outs


def layer_norm(x, gain, bias):
    xf = x.astype(jnp.float32)
    mu = jnp.mean(xf, axis=-1, keepdims=True)
    var = jnp.mean(jnp.square(xf - mu), axis=-1, keepdims=True)
    y = (xf - mu) * lax.rsqrt(var + LN_EPS) * gain.astype(jnp.float32) + bias.astype(jnp.float32)
    return y.astype(x.dtype)


def rope_partial(x, pos):
    d = x.shape[-1]
    rot = d // ROPE_FRACTION
    half = rot // 2
    inv_freq = ROPE_THETA ** (-jnp.arange(half, dtype=jnp.float32) * 2.0 / rot)
    ang = pos.astype(jnp.float32)[:, None] * inv_freq[None, :]
    cos = jnp.cos(ang)[:, None, :]
    sin = jnp.sin(ang)[:, None, :]
    x1 = x[..., :half].astype(jnp.float32)
    x2 = x[..., half:rot].astype(jnp.float32)
    out = jnp.concatenate([(x1 * cos - x2 * sin).astype(x.dtype),
                           (x2 * cos + x1 * sin).astype(x.dtype),
                           x[..., rot:]], axis=-1)
    return out


def gla_mixer(q, k, v, a_low, r, w_a2, b_a, norm_gain):
    B, L, _ = q.shape
    dt = q.dtype
    f32 = jnp.float32
    q = q.reshape(B, L, GLA_HEADS, GLA_DK).astype(f32) * (GLA_DK ** -0.5)
    k = k.reshape(B, L, GLA_HEADS, GLA_DK).astype(f32)
    v = v.reshape(B, L, GLA_HEADS, GLA_DV).astype(f32)
    g = jax.nn.log_sigmoid((a_low @ w_a2 + b_a).astype(f32)) / GLA_TAU
    g = g.reshape(B, L, GLA_HEADS, GLA_DK)
    n_chunks = L // GLA_CHUNK

    def to_chunks(t):
        return t.reshape(B, n_chunks, GLA_CHUNK, GLA_HEADS, t.shape[-1]).swapaxes(0, 1)

    causal = jnp.tril(jnp.ones((GLA_CHUNK, GLA_CHUNK), dtype=bool))

    def step(state, inp):
        qc, kc, vc, gc = inp
        b = jnp.cumsum(gc, axis=1)
        b_last = b[:, -1]
        o_inter = jnp.einsum('bchk,bhkv->bchv', qc * jnp.exp(b), state)
        diff = b[:, :, None] - b[:, None, :]
        decay = jnp.exp(jnp.where(causal[None, :, :, None, None], diff, -jnp.inf))
        scores = jnp.einsum('bihk,bjhk,bijhk->bhij', qc, kc, decay)
        o_intra = jnp.einsum('bhij,bjhv->bihv', scores, vc)
        new_state = jnp.exp(b_last)[..., None] * state + jnp.einsum(
            'bjhk,bjhv->bhkv', kc * jnp.exp(b_last[:, None] - b), vc)
        return new_state, o_inter + o_intra

    state0 = jnp.zeros((B, GLA_HEADS, GLA_DK, GLA_DV), f32)
    _, o = lax.scan(step, state0, (to_chunks(q), to_chunks(k), to_chunks(v), to_chunks(g)))
    o = o.swapaxes(0, 1).reshape(B, L, GLA_HEADS, GLA_DV)
    o = o * lax.rsqrt(jnp.mean(jnp.square(o), axis=-1, keepdims=True) + RMS_EPS) * norm_gain.astype(f32)
    y = o.reshape(B, L, GLA_HEADS * GLA_DV) * jax.nn.silu(r.astype(f32))
    return y.astype(dt)


def dsa_mixer(q, k, v, qi, ki, wi, pos):
    B, L, _ = q.shape
    dt = q.dtype
    f32 = jnp.float32
    q = rope_partial(q.reshape(B, L, DSA_HEADS, DSA_HEAD_DIM), pos)
    k = rope_partial(k.reshape(B, L, 1, DSA_HEAD_DIM), pos)[:, :, 0]
    v = v.reshape(B, L, DSA_HEAD_DIM)
    qi = rope_partial(qi.reshape(B, L, IDX_HEADS, IDX_DIM), pos)
    ki = rope_partial(ki.reshape(B, L, 1, IDX_DIM), pos)[:, :, 0]
    wi = wi * (IDX_HEADS ** -0.5)
    top_k = min(DSA_TOPK_MAX, L // 4)
    n_blocks = L // DSA_QBLOCK
    key_pos = jnp.arange(L, dtype=jnp.int32)

    def blocks(t):
        return t.reshape((B, n_blocks, DSA_QBLOCK) + t.shape[2:]).swapaxes(0, 1)

    def gather_rows(table, idx):
        return table[idx]

    def one_block(args):
        q_b, qi_b, wi_b, t0 = args
        q_pos = t0 + jnp.arange(DSA_QBLOCK, dtype=jnp.int32)
        idx_logits = jnp.einsum('bqhd,bsd->bqhs', qi_b, ki) * (IDX_DIM ** -0.5)
        score = jnp.einsum('bqhs,bqh->bqs', jax.nn.relu(idx_logits), wi_b).astype(f32)
        admissible = key_pos[None, :] <= q_pos[:, None]
        score = jnp.where(admissible[None], score, -jnp.inf)
        _, sel = lax.top_k(score, top_k)
        k_sel = jax.vmap(gather_rows)(k, sel)
        v_sel = jax.vmap(gather_rows)(v, sel)
        valid = sel <= q_pos[None, :, None]
        logits = jnp.einsum('bqhd,bqkd->bqhk', q_b, k_sel).astype(f32) * (DSA_HEAD_DIM ** -0.5)
        logits = jnp.where(valid[:, :, None, :], logits, -jnp.inf)
        p = jax.nn.softmax(logits, axis=-1).astype(dt)
        return jnp.einsum('bqhk,bqkd->bqhd', p, v_sel)

    t0s = jnp.arange(n_blocks, dtype=jnp.int32) * DSA_QBLOCK
    o = lax.map(one_block, (blocks(q), blocks(qi), blocks(wi), t0s))
    return o.swapaxes(0, 1).reshape(B, L, DSA_HEADS * DSA_HEAD_DIM)


def causal_dwconv(u, w, b):
    L = u.shape[1]
    up = jnp.pad(u, ((0, 0), (CONV_WIDTH - 1, 0), (0, 0)))
    out = b
    for j in range(CONV_WIDTH):
        out = out + up[:, j:j + L] * w[j]
    return out


def setup_inputs(seed: int = 0) -> dict:
    key = jax.random.key(seed)
    ks = jax.random.split(key, 16)
    f32 = jnp.float32
    nrm = lambda k, shape, scale: jax.random.normal(k, shape, f32) * scale
    return {
        "x": jax.random.normal(ks[0], (BATCH, SEQ, D_MODEL), f32),
        "w_in": nrm(ks[1], (DEPTH, D_MODEL, D_IN), D_MODEL ** -0.5),
        "w_gla_a2": nrm(ks[2], (DEPTH, GLA_GATE_RANK, GLA_HEADS * GLA_DK), GLA_GATE_RANK ** -0.5),
        "b_gla_a": nrm(ks[3], (DEPTH, GLA_HEADS * GLA_DK), 0.1),
        "gla_norm_gain": 1.0 + nrm(ks[4], (DEPTH, GLA_HEADS, GLA_DV), 0.02),
        "w_branch": nrm(ks[5], (DEPTH, N_BRANCH, BRANCH_WIDTH, D_MODEL), BRANCH_WIDTH ** -0.5 * DEEPNORM_BETA),
        "w_o": nrm(ks[6], (DEPTH, D_MODEL, D_MODEL), D_MODEL ** -0.5 * DEEPNORM_BETA),
        "ln1_gain": 1.0 + nrm(ks[7], (DEPTH, D_MODEL), 0.02),
        "ln1_bias": nrm(ks[8], (DEPTH, D_MODEL), 0.02),
        "w_up": nrm(ks[9], (DEPTH, D_MODEL, 2 * D_FF), D_MODEL ** -0.5),
        "conv_w": nrm(ks[10], (DEPTH, CONV_WIDTH, 2 * D_FF), CONV_WIDTH ** -0.5),
        "conv_b": nrm(ks[11], (DEPTH, 2 * D_FF), 0.02),
        "w_down": nrm(ks[12], (DEPTH, D_FF, D_MODEL), D_FF ** -0.5 * DEEPNORM_BETA),
        "ln2_gain": 1.0 + nrm(ks[13], (DEPTH, D_MODEL), 0.02),
        "ln2_bias": nrm(ks[14], (DEPTH, D_MODEL), 0.02),
    }


def reference(x, w_in, w_gla_a2, b_gla_a, gla_norm_gain, w_branch, w_o, ln1_gain, ln1_bias,
              w_up, conv_w, conv_b, w_down, ln2_gain, ln2_bias):
    B, L, D = x.shape
    pos = jnp.arange(L, dtype=jnp.int32)
    for layer in range(DEPTH):
        h = x @ w_in[layer]
        (gq, gk, gv, ga, gr, dq, dk, dv, iq, ik, iw, gate) = split_cols(h, IN_WIDTHS)
        y_gla = gla_mixer(gq, gk, gv, ga, gr, w_gla_a2[layer], b_gla_a[layer], gla_norm_gain[layer])
        y_dsa = dsa_mixer(dq, dk, dv, iq, ik, iw, pos)
        branches = jnp.stack([y_gla, y_dsa], axis=2)
        proj = jnp.einsum('bsnc,ncd->bsnd', branches, w_branch[layer])
        gates = jax.nn.sigmoid(gate.reshape(B, L, N_BRANCH, D))
        mixed = jnp.sum(gates * proj, axis=2) @ w_o[layer]
        x = layer_norm(DEEPNORM_ALPHA * x + mixed, ln1_gain[layer], ln1_bias[layer])
        u = causal_dwconv(x @ w_up[layer], conv_w[layer], conv_b[layer])
        u_gate, u_val = u[..., :D_FF], u[..., D_FF:]
        f = (jax.nn.silu(u_gate) * u_val) @ w_down[layer]
        x = layer_norm(DEEPNORM_ALPHA * x + f, ln2_gain[layer], ln2_bias[layer])
    return x
```

```python
import functools
import math

import numpy as np
import jax
import jax.numpy as jnp
from jax import lax
from jax.experimental import pallas as pl
from jax.experimental.pallas import tpu as pltpu

F32 = jnp.float32
BF16 = jnp.bfloat16
I32 = jnp.int32

GLA_HEADS = 4
GLA_DK = 64
GLA_DV = 128
GLA_GATE_RANK = 16
GLA_TAU = 16.0
DSA_HEADS = 8
DSA_HEAD_DIM = 64
IDX_HEADS = 8
IDX_DIM = 32
DSA_TOPK_MAX = 256
ROPE_THETA = 500000.0
ROPE_FRACTION = 4
N_BRANCH = 2
BRANCH_WIDTH = 512
CONV_WIDTH = 3
LN_EPS = 1e-5
RMS_EPS = 1e-6

LANES = 128
SUBLANES = 8
VMEM_LIMIT_BYTES = 56 * 1024 * 1024

GLA_CHUNK = 64
GLA_SUB = 16
INT_MIN = -(2 ** 31)
NEG_BIG = -1e30


def _cparams(sem):
    return pltpu.CompilerParams(dimension_semantics=sem, vmem_limit_bytes=VMEM_LIMIT_BYTES)


def _log2(n):
    assert n > 0 and n & (n - 1) == 0, f"{n} must be a power of two"
    return n.bit_length() - 1


def _div_pow2(x, n):
    return lax.shift_right_logical(x, jnp.int32(_log2(n)))


def _mod_pow2(x, n):
    _log2(n)
    return x & (n - 1)


def _dot(a, b):
    return jnp.dot(a, b, preferred_element_type=F32)


def _dot_nt(a, b):
    return lax.dot_general(a, b, (((1,), (1,)), ((), ())), preferred_element_type=F32)


def _dot_tn(a, b):
    return lax.dot_general(a, b, (((0,), (0,)), ((), ())), preferred_element_type=F32)


def _rope_table_kernel(invf_ref, c_ref, sa_ref, sb_ref, *, rows, periods, halves):
    t0 = pl.program_id(0) * rows
    pos = (t0 + lax.broadcasted_iota(I32, (rows, LANES), 0)).astype(F32)
    lane = lax.broadcasted_iota(I32, (rows, LANES), 1)
    for p, (period, half) in enumerate(zip(periods, halves)):
        ang = pos * invf_ref[p]
        c = jnp.cos(ang)
        s = jnp.sin(ang)
        m = _mod_pow2(lane, period)
        is1 = m < half
        is2 = (m >= half) & (m < 2 * half)
        c_ref[p] = jnp.where(is1 | is2, c, 1.0)
        sa_ref[p] = jnp.where(is1, -s, 0.0)
        sb_ref[p] = jnp.where(is2, s, 0.0)


def _rope_tables(L, invf):
    rows = min(L, 512)
    n = invf.shape[0]
    spec = pl.BlockSpec((n, rows, LANES), lambda i: (0, i, 0))
    shp = jax.ShapeDtypeStruct((n, L, LANES), F32)
    return pl.pallas_call(
        functools.partial(_rope_table_kernel, rows=rows, periods=(DSA_HEAD_DIM, IDX_DIM),
                          halves=(DSA_HEAD_DIM // ROPE_FRACTION // 2, IDX_DIM // ROPE_FRACTION // 2)),
        grid=(L // rows,),
        in_specs=[pl.BlockSpec((n, 1, LANES), lambda i: (0, 0, 0))],
        out_specs=[spec, spec, spec],
        out_shape=[shp, shp, shp],
        compiler_params=_cparams(("arbitrary",)),
        name="rope_tables",
    )(invf)


def _apply_rope(x, c, sa, sb, half, lane_limit=None):
    outs = []
    for g in range(x.shape[1] // LANES):
        xs = x[:, g * LANES:(g + 1) * LANES]
        up = pltpu.roll(xs, LANES - half, axis=1)
        dn = pltpu.roll(xs, half, axis=1)
        y = xs * c + up * sa + dn * sb
        if lane_limit is not None:
            lane = lax.broadcasted_iota(I32, xs.shape, 1)
            y = jnp.where(lane < lane_limit, y, xs)
        outs.append(y)
    return outs[0] if len(outs) == 1 else jnp.concatenate(outs, axis=1)


def _in_proj_kernel(x_ref, wgla_ref, wga_ref, wa2_ref, ba_ref, wdq_ref, wdkv_ref, wiq_ref, wik_ref, wiw_ref,
                    c_ref, sa_ref, sb_ref,
                    gla_ref, g_ref, dq_ref, dkv_ref, iq_ref, ik_ref, iw_ref):
    xb = x_ref[0].astype(BF16)
    hg = _dot(xb, wgla_ref[...])
    nqk = GLA_HEADS * GLA_DK
    gla_ref[0, :, 0:nqk] = (hg[:, 0:nqk] * (GLA_DK ** -0.5)).astype(BF16)
    gla_ref[0, :, nqk:] = hg[:, nqk:].astype(BF16)
    a_low = _dot(xb, wga_ref[...])
    z = _dot(a_low.astype(BF16), wa2_ref[...]) + ba_ref[...]
    g_ref[0] = (jnp.minimum(z, 0.0) - jnp.log1p(jnp.exp(-jnp.abs(z)))) * (1.0 / GLA_TAU)
    c64, sa64, sb64 = c_ref[0], sa_ref[0], sb_ref[0]
    c32, sa32, sb32 = c_ref[1], sa_ref[1], sb_ref[1]
    h64 = DSA_HEAD_DIM // ROPE_FRACTION // 2
    h32 = IDX_DIM // ROPE_FRACTION // 2
    dq = _dot(xb, wdq_ref[...])
    dq_ref[0] = (_apply_rope(dq, c64, sa64, sb64, h64) * (DSA_HEAD_DIM ** -0.5)).astype(BF16)
    dkv = _dot(xb, wdkv_ref[...])
    dkv_ref[0] = _apply_rope(dkv, c64, sa64, sb64, h64, lane_limit=DSA_HEAD_DIM).astype(BF16)
    iq = _dot(xb, wiq_ref[...])
    iq_ref[0] = _apply_rope(iq, c32, sa32, sb32, h32).astype(BF16)
    ik = _dot(xb, wik_ref[...])
    ik_ref[0] = _apply_rope(ik, c32, sa32, sb32, h32).astype(BF16)
    iw_ref[0] = _dot(xb, wiw_ref[...]) * ((IDX_HEADS ** -0.5) * (IDX_DIM ** -0.5))


def _in_proj(x, wts, tabs, rows):
    B, L, D = x.shape
    wgla, wga, wa2, ba, wdq, wdkv, wiq, wik, wiw = wts
    c, sa, sb = tabs
    n_l = L // rows

    def full(a):
        return pl.BlockSpec(a.shape, lambda i, b, _n=a.ndim: (0,) * _n)

    def rowblk(w):
        return pl.BlockSpec((1, rows, w), lambda i, b: (b, i, 0))

    tab = pl.BlockSpec((2, rows, LANES), lambda i, b: (0, i, 0))
    widths = (wgla.shape[1], GLA_HEADS * GLA_DK, wdq.shape[1], wdkv.shape[1], wiq.shape[1], wik.shape[1], wiw.shape[1])
    dtypes = (BF16, F32, BF16, BF16, BF16, BF16, F32)
    return pl.pallas_call(
        _in_proj_kernel,
        grid=(n_l, B),
        in_specs=[rowblk(D)] + [full(w) for w in wts] + [tab, tab, tab],
        out_specs=[rowblk(w) for w in widths],
        out_shape=[jax.ShapeDtypeStruct((B, L, w), dt) for w, dt in zip(widths, dtypes)],
        compiler_params=_cparams(("arbitrary", "arbitrary")),
        name="in_proj",
    )(x, *wts, c, sa, sb)


def _gla_kernel(qkvr_ref, g_ref, gain_ref, tril_ref, expand_ref, y_ref, st_ref, *, rows):
    C, S = GLA_CHUNK, GLA_SUB
    H, DK, DV = GLA_HEADS, GLA_DK, GLA_DV
    HK, HV = H * DK, H * DV

    @pl.when(pl.program_id(1) == 0)
    def _():
        st_ref[...] = jnp.zeros_like(st_ref)

    ri = lax.broadcasted_iota(I32, (C, HK), 0)
    li = lax.broadcasted_iota(I32, (C, HK), 1)
    head_of_lane = _div_pow2(li, DK)
    sr = _div_pow2(lax.broadcasted_iota(I32, (HV, HK), 0), DV)
    scol = _div_pow2(lax.broadcasted_iota(I32, (HV, HK), 1), DK)
    st_mask = sr == scol
    tril = tril_ref[...]
    expand = expand_ref[...]
    gain = gain_ref[...]

    def chunk(ci, carry):
        r0 = pl.multiple_of(ci * C, C)
        blk = qkvr_ref[0, pl.ds(r0, C), :]
        q = blk[:, 0:HK].astype(F32)
        k = blk[:, HK:2 * HK].astype(F32)
        v = blk[:, 2 * HK:2 * HK + HV].astype(F32)
        r = blk[:, 2 * HK + HV:].astype(F32)
        g = g_ref[0, pl.ds(r0, C), :]
        g1 = g.astype(BF16)
        rem = g - g1.astype(F32)
        g2 = rem.astype(BF16)
        g3 = (rem - g2.astype(F32)).astype(BF16)
        b = _dot(tril, g1) + _dot(tril, g2) + _dot(tril, g3)
        b_last = b[C - 1:C, :]
        st = st_ref[...]
        o = _dot_nt((q * jnp.exp(b)).astype(BF16), st.astype(BF16))
        p_acc = jnp.zeros((H * C, C), F32)
        for I in range(1, C // S):
            beta = b[I * S - 1:I * S, :]
            in_I = (ri >= I * S) & (ri < (I + 1) * S)
            qI = jnp.where(in_I, q * jnp.exp(jnp.minimum(b - beta, 0.0)), 0.0)
            kI = jnp.where(ri < I * S, k * jnp.exp(jnp.minimum(beta - b, 0.0)), 0.0)
            q_st = jnp.concatenate([jnp.where(head_of_lane == h, qI, 0.0) for h in range(H)], axis=0)
            p_acc = p_acc + _dot_nt(q_st.astype(BF16), kI.astype(BF16))
        p_b = p_acc.astype(BF16)
        vb = v.astype(BF16)
        o_off = jnp.concatenate(
            [_dot(p_b[h * C:(h + 1) * C, :], vb[:, h * DV:(h + 1) * DV]) for h in range(H)], axis=1)
        o = o + o_off
        GRP = 4
        for d0 in range(0, S, GRP):
            ts = []
            for d in range(d0, d0 + GRP):
                if d == 0:
                    ts.append(q * k)
                else:
                    kd = pltpu.roll(k, d, axis=0)
                    bd = pltpu.roll(b, d, axis=0)
                    valid = _mod_pow2(ri, S) >= d
                    ts.append(q * kd * jnp.exp(jnp.where(valid, jnp.minimum(b - bd, 0.0), -jnp.inf)))
            s_all = _dot(jnp.concatenate(ts, axis=0).astype(BF16), expand)
            for n, d in enumerate(range(d0, d0 + GRP)):
                vd = v if d == 0 else pltpu.roll(v, d, axis=0)
                o = o + s_all[n * C:(n + 1) * C, :] * vd
        k_end = k * jnp.exp(b_last - b)
        upd = _dot_tn(vb, k_end.astype(BF16))
        st_ref[...] = st * jnp.exp(b_last) + jnp.where(st_mask, upd, 0.0)
        outs = []
        for h in range(H):
            oh = o[:, h * DV:(h + 1) * DV]
            ms = jnp.mean(oh * oh, axis=-1, keepdims=True)
            outs.append(oh * lax.rsqrt(ms + RMS_EPS))
        on = jnp.concatenate(outs, axis=1) * gain
        y = on * (r * jax.nn.sigmoid(r))
        y_ref[0, pl.ds(r0, C), :] = y.astype(y_ref.dtype)
        return carry

    lax.fori_loop(0, rows // C, chunk, 0)


def _gla(qkvr, g, gain, rows):
    B, L, W = qkvr.shape
    HK, HV = GLA_HEADS * GLA_DK, GLA_HEADS * GLA_DV
    C = GLA_CHUNK
    tril = jnp.asarray(np.tril(np.ones((C, C), np.float32)), BF16)
    expand = jnp.asarray(np.kron(np.eye(GLA_HEADS, dtype=np.float32), np.ones((GLA_DK, GLA_DV), np.float32)), BF16)
    return pl.pallas_call(
        functools.partial(_gla_kernel, rows=rows),
        grid=(B, L // rows),
        in_specs=[pl.BlockSpec((1, rows, W), lambda b, i: (b, i, 0)),
                  pl.BlockSpec((1, rows, HK), lambda b, i: (b, i, 0)),
                  pl.BlockSpec((1, HV), lambda b, i: (0, 0)),
                  pl.BlockSpec((C, C), lambda b, i: (0, 0)),
                  pl.BlockSpec((HK, HV), lambda b, i: (0, 0))],
        out_specs=pl.BlockSpec((1, rows, HV), lambda b, i: (b, i, 0)),
        out_shape=jax.ShapeDtypeStruct((B, L, HV), BF16),
        scratch_shapes=[pltpu.VMEM((HV, HK), F32)],
        compiler_params=_cparams(("arbitrary", "arbitrary")),
        name="gla",
    )(qkvr, g, gain, tril, expand)


def _dsa_kernel(dq_ref, kv_ref, iq_ref, ik_ref, iw_ref, o_ref,
                key_ref, w_ref, qe_ref, qi_ref, m_ref, l_ref, acc_ref, *, tq, tk, top_k):
    NH, DH = DSA_HEADS, DSA_HEAD_DIM
    i = pl.program_id(1)
    t0 = i * tq
    nkb = (t0 + tq + tk - 1) // tk
    lane_q = lax.broadcasted_iota(I32, (tq, LANES), 1)

    iw = iw_ref[0]
    iq = iq_ref[0].astype(F32)
    head_of_iq_lane = _div_pow2(lax.broadcasted_iota(I32, iq.shape, 1), IDX_DIM)
    for h in range(IDX_HEADS):
        w_ref[h] = jnp.broadcast_to(iw[:, h:h + 1], (tq, LANES))
        qi_ref[h] = jnp.where(head_of_iq_lane == h, iq, 0.0).astype(BF16)
    for p in range(NH // 2):
        pair = dq_ref[0, :, p * LANES:(p + 1) * LANES].astype(F32)
        swapped = pltpu.roll(pair, DH, axis=1)
        qe_ref[2 * p] = jnp.where(lane_q < DH, pair, 0.0).astype(BF16)
        qe_ref[2 * p + 1] = jnp.where(lane_q < DH, swapped, 0.0).astype(BF16)

    q_pos = t0 + lax.broadcasted_iota(I32, (tq, tk), 0)
    k_off = lax.broadcasted_iota(I32, (tq, tk), 1)

    def score_block(j, carry):
        k0 = pl.multiple_of(j * tk, tk)
        kib = ik_ref[0, pl.ds(k0, tk), :]
        sc = jnp.zeros((tq, tk), F32)
        for h in range(IDX_HEADS):
            lg = _dot_nt(qi_ref[h], kib)
            sc = sc + jnp.maximum(lg, 0.0) * jnp.tile(w_ref[h], (1, tk // LANES))
        sc = sc + 0.0
        bits = pltpu.bitcast(sc, I32)
        key = bits ^ ((bits >> 31) & 0x7FFFFFFF)
        key_ref[j] = jnp.where(k0 + k_off <= q_pos, key, INT_MIN)
        return carry

    lax.fori_loop(0, nkb, score_block, 0)

    def count_ge(cand):
        cand_t = jnp.tile(cand, (1, tk // LANES))

        def body(j, cnt):
            hit = jnp.where(key_ref[j] >= cand_t, 1.0, 0.0)
            for c in range(tk // LANES):
                cnt = cnt + hit[:, c * LANES:(c + 1) * LANES]
            return cnt

        cnt = lax.fori_loop(0, nkb, body, jnp.zeros((tq, LANES), F32))
        return jnp.broadcast_to(jnp.sum(cnt, axis=1, keepdims=True), (tq, LANES))

    def search_bit(it, thr):
        cand = thr + lax.shift_left(jnp.int32(1), 31 - it)
        return jnp.where(count_ge(cand) >= top_k, cand, thr)

    thr = lax.fori_loop(0, 32, search_bit, jnp.full((tq, LANES), INT_MIN, I32))
    thr = jnp.maximum(thr, INT_MIN + 1)
    c_ge = count_ge(thr)
    c_gt = count_ge(thr + 1)
    excess = jnp.max(jnp.where(c_ge > top_k, 1.0, 0.0))

    @pl.when(excess > 0.0)
    def _():
        need = top_k - c_gt
        thr_t = jnp.tile(thr, (1, tk // LANES))
        nbits = max(1, int(math.ceil(math.log2(key_ref.shape[0] * tk))))

        def count_eq_le(idx_lim):
            lim_t = jnp.tile(idx_lim, (1, tk // LANES))

            def body(j, cnt):
                kidx = j * tk + k_off
                hit = jnp.where((key_ref[j] == thr_t) & (kidx <= lim_t), 1.0, 0.0)
                for c in range(tk // LANES):
                    cnt = cnt + hit[:, c * LANES:(c + 1) * LANES]
                return cnt

            cnt = lax.fori_loop(0, nkb, body, jnp.zeros((tq, LANES), F32))
            return jnp.broadcast_to(jnp.sum(cnt, axis=1, keepdims=True), (tq, LANES))

        def idx_bit(it, lo):
            trial = lo + lax.shift_left(jnp.int32(1), nbits - 1 - it)
            return jnp.where(count_eq_le(trial - 1) < need, trial, lo)

        cut = lax.fori_loop(0, nbits, idx_bit, jnp.zeros((tq, LANES), I32))
        cut_t = jnp.tile(cut, (1, tk // LANES))

        def drop(j, carry):
            kidx = j * tk + k_off
            key = key_ref[j]
            key_ref[j] = jnp.where((key == thr_t) & (kidx > cut_t), INT_MIN, key)
            return carry

        lax.fori_loop(0, nkb, drop, 0)

    m_ref[...] = jnp.full(m_ref.shape, NEG_BIG, F32)
    l_ref[...] = jnp.zeros(l_ref.shape, F32)
    acc_ref[...] = jnp.zeros(acc_ref.shape, F32)
    thr_t = jnp.tile(thr, (1, tk // LANES))

    def attend(j, carry):
        k0 = pl.multiple_of(j * tk, tk)
        kvb = kv_ref[0, pl.ds(k0, tk), :]
        sel = key_ref[j] >= thr_t
        for h in range(NH):
            s = jnp.where(sel, _dot_nt(qe_ref[h], kvb), NEG_BIG)
            m_old = m_ref[h]
            m_new = jnp.maximum(m_old, jnp.broadcast_to(jnp.max(s, axis=1, keepdims=True), (tq, LANES)))
            alpha = jnp.exp(m_old - m_new)
            p = jnp.exp(s - jnp.tile(m_new, (1, tk // LANES)))
            l_ref[h] = l_ref[h] * alpha + jnp.broadcast_to(jnp.sum(p, axis=1, keepdims=True), (tq, LANES))
            acc_ref[h] = acc_ref[h] * alpha + _dot(p.astype(BF16), kvb)
            m_ref[h] = m_new
        return carry

    lax.fori_loop(0, nkb, attend, 0)

    for p in range(NH // 2):
        even = acc_ref[2 * p] / l_ref[2 * p]
        odd = acc_ref[2 * p + 1] / l_ref[2 * p + 1]
        o_ref[0, :, p * LANES:(p + 1) * LANES] = jnp.where(
            lane_q < DH, pltpu.roll(even, DH, axis=1), odd).astype(o_ref.dtype)


def _dsa(dq, dkv, iq, ik, iw, tq, tk):
    B, L, _ = dq.shape
    top_k = min(DSA_TOPK_MAX, L // 4)
    nkb_max = L // tk
    return pl.pallas_call(
        functools.partial(_dsa_kernel, tq=tq, tk=tk, top_k=float(top_k)),
        grid=(B, L // tq),
        in_specs=[pl.BlockSpec((1, tq, dq.shape[2]), lambda b, i: (b, i, 0)),
                  pl.BlockSpec((1, L, dkv.shape[2]), lambda b, i: (b, 0, 0)),
                  pl.BlockSpec((1, tq, iq.shape[2]), lambda b, i: (b, i, 0)),
                  pl.BlockSpec((1, L, ik.shape[2]), lambda b, i: (b, 0, 0)),
                  pl.BlockSpec((1, tq, iw.shape[2]), lambda b, i: (b, i, 0))],
        out_specs=pl.BlockSpec((1, tq, dq.shape[2]), lambda b, i: (b, i, 0)),
        out_shape=jax.ShapeDtypeStruct(dq.shape, BF16),
        scratch_shapes=[pltpu.VMEM((nkb_max, tq, tk), I32),
                        pltpu.VMEM((IDX_HEADS, tq, LANES), F32),
                        pltpu.VMEM((DSA_HEADS, tq, LANES), BF16),
                        pltpu.VMEM((IDX_HEADS, tq, iq.shape[2]), BF16),
                        pltpu.VMEM((DSA_HEADS, tq, LANES), F32),
                        pltpu.VMEM((DSA_HEADS, tq, LANES), F32),
                        pltpu.VMEM((DSA_HEADS, tq, LANES), F32)],
        compiler_params=_cparams(("arbitrary", "arbitrary")),
        name="dsa",
    )(dq, dkv, iq, ik, iw)


def _layer_norm(z, gain, bias):
    mu = jnp.mean(z, axis=-1, keepdims=True)
    zc = z - mu
    var = jnp.mean(zc * zc, axis=-1, keepdims=True)
    return zc * lax.rsqrt(var + LN_EPS) * gain + bias


def _merge_kernel(x_ref, yg_ref, yd_ref, wgate_ref, wbr_ref, wo_ref, gain_ref, bias_ref, o_ref, *, alpha):
    x = x_ref[0]
    D = x.shape[1]
    gates = jax.nn.sigmoid(_dot(x.astype(BF16), wgate_ref[...]))
    pg = _dot(yg_ref[0], wbr_ref[0])
    pd = _dot(yd_ref[0], wbr_ref[1])
    m = gates[:, 0:D] * pg + gates[:, D:] * pd
    mixed = _dot(m.astype(BF16), wo_ref[...])
    o_ref[0] = _layer_norm(alpha * x + mixed, gain_ref[...], bias_ref[...])


def _merge(x, yg, yd, wgate, wbr, wo, gain, bias, rows, alpha):
    B, L, D = x.shape

    def full(a):
        return pl.BlockSpec(a.shape, lambda b, i, _n=a.ndim: (0,) * _n)

    def rowblk(w):
        return pl.BlockSpec((1, rows, w), lambda b, i: (b, i, 0))

    return pl.pallas_call(
        functools.partial(_merge_kernel, alpha=alpha),
        grid=(B, L // rows),
        in_specs=[rowblk(D), rowblk(yg.shape[2]), rowblk(yd.shape[2]),
                  full(wgate), full(wbr), full(wo), full(gain), full(bias)],
        out_specs=rowblk(D),
        out_shape=jax.ShapeDtypeStruct((B, L, D), F32),
        compiler_params=_cparams(("arbitrary", "arbitrary")),
        name="merge_ln1",
    )(x, yg, yd, wgate, wbr, wo, gain, bias)


def _ffn_kernel(x_ref, wup_ref, cw_ref, cb_ref, wdn_ref, gain_ref, bias_ref, o_ref, hist_ref, *, alpha, n_chunks, cw):
    rows = x_ref.shape[1]

    @pl.when(pl.program_id(1) == 0)
    def _():
        hist_ref[...] = jnp.zeros_like(hist_ref)

    x = x_ref[0]
    xb = x.astype(BF16)
    ri = lax.broadcasted_iota(I32, (rows, 2 * cw), 0)

    def chunk(c, f):
        u = _dot(xb, wup_ref[c])
        hist = hist_ref[c]
        h1 = jnp.broadcast_to(hist[SUBLANES - 1:SUBLANES, :], u.shape)
        h2 = jnp.broadcast_to(hist[SUBLANES - 2:SUBLANES - 1, :], u.shape)
        u1 = jnp.where(ri == 0, h1, pltpu.roll(u, 1, axis=0))
        u2 = jnp.where(ri == 0, h2, jnp.where(ri == 1, h1, pltpu.roll(u, 2, axis=0)))
        hist_ref[c] = u[rows - SUBLANES:, :]
        w = cw_ref[c]
        cu = cb_ref[c] + u2 * w[0:1, :] + u1 * w[1:2, :] + u * w[2:3, :]
        ug = cu[:, 0:cw]
        act = (ug * jax.nn.sigmoid(ug)) * cu[:, cw:]
        return f + _dot(act.astype(BF16), wdn_ref[c])

    f = lax.fori_loop(0, n_chunks, chunk, jnp.zeros(x.shape, F32))
    o_ref[0] = _layer_norm(alpha * x + f, gain_ref[...], bias_ref[...])


def _ffn(x, wup, cwt, cbs, wdn, gain, bias, rows, alpha):
    B, L, D = x.shape
    n_chunks, _, cw2 = wup.shape

    def full(a):
        return pl.BlockSpec(a.shape, lambda b, i, _n=a.ndim: (0,) * _n)

    rowblk = pl.BlockSpec((1, rows, D), lambda b, i: (b, i, 0))
    return pl.pallas_call(
        functools.partial(_ffn_kernel, alpha=alpha, n_chunks=n_chunks, cw=cw2 // 2),
        grid=(B, L // rows),
        in_specs=[rowblk, full(wup), full(cwt), full(cbs), full(wdn), full(gain), full(bias)],
        out_specs=rowblk,
        out_shape=jax.ShapeDtypeStruct((B, L, D), F32),
        scratch_shapes=[pltpu.VMEM((n_chunks, SUBLANES, cw2), F32)],
        compiler_params=_cparams(("arbitrary", "arbitrary")),
        name="ffn_ln2",
    )(x, wup, cwt, cbs, wdn, gain, bias)


def _pad_cols(w, width):
    return jnp.pad(w, ((0, 0), (0, width - w.shape[1])))


def _ffn_chunk_width(d_ff):
    for cw in (512, 384, 256, 128):
        if d_ff % cw == 0:
            return cw
    raise ValueError(f"d_ff={d_ff} must be a multiple of {LANES}")


def _token_mixers(x, w_in, w_a2, b_a, gla_gain):
    B, L, D = x.shape
    nqk = GLA_HEADS * GLA_DK
    nv = GLA_HEADS * GLA_DV
    ndq = DSA_HEADS * DSA_HEAD_DIM
    niq = IDX_HEADS * IDX_DIM
    widths = (nqk, nqk, nv, GLA_GATE_RANK, nv, ndq, DSA_HEAD_DIM, DSA_HEAD_DIM, niq, IDX_DIM, IDX_HEADS, N_BRANCH * D)
    offs = np.concatenate([[0], np.cumsum(widths)])
    assert w_in.shape == (D, int(offs[-1]))
    col = lambda n: w_in[:, int(offs[n]):int(offs[n + 1])]
    w_gq, w_gk, w_gv, w_ga, w_gr, w_dq, w_dk, w_dv, w_iq, w_ik, w_iw, w_gate = [col(n) for n in range(12)]
    wts = (
        jnp.concatenate([w_gq, w_gk, w_gv, w_gr], axis=1).astype(BF16),
        _pad_cols(w_ga, LANES).astype(BF16),
        jnp.pad(w_a2, ((0, LANES - GLA_GATE_RANK), (0, 0))).astype(BF16),
        b_a.reshape(1, nqk).astype(F32),
        w_dq.astype(BF16),
        jnp.concatenate([w_dk, w_dv], axis=1).astype(BF16),
        w_iq.astype(BF16),
        jnp.tile(w_ik, (1, IDX_HEADS)).astype(BF16),
        _pad_cols(w_iw, LANES).astype(BF16),
    )
    def lane_freqs(head_dim):
        rot = head_dim // ROPE_FRACTION
        half = rot // 2
        inv = ROPE_THETA ** (-jnp.arange(half, dtype=F32) * 2.0 / rot)
        m = np.arange(LANES) % head_dim
        return jnp.where(jnp.asarray(m < rot), inv[jnp.asarray(m % half)], 0.0)

    invf = jnp.stack([lane_freqs(DSA_HEAD_DIM), lane_freqs(IDX_DIM)]).reshape(2, 1, LANES)
    tabs = _rope_tables(L, invf)

    rows_proj = min(L, 512)
    gla_in, g, dq, dkv, iq, ik, iw = _in_proj(x, wts, tabs, rows_proj)
    y_gla = _gla(gla_in, g, gla_gain.reshape(1, nv).astype(F32), rows=min(L, 512))
    tq = min(L, 128)
    tk = min(L, 512)
    y_dsa = _dsa(dq, dkv, iq, ik, iw, tq, tk)
    return y_gla, y_dsa, w_gate


def _layer(x, w_in, w_a2, b_a, gla_gain, w_branch, w_o, ln1_g, ln1_b, w_up, conv_w, conv_b, w_down, ln2_g, ln2_b,
           depth):
    B, L, D = x.shape
    alpha = (2.0 * depth) ** 0.25
    y_gla, y_dsa, w_gate = _token_mixers(x, w_in, w_a2, b_a, gla_gain)
    x1 = _merge(x, y_gla, y_dsa, w_gate.astype(BF16), w_branch.astype(BF16), w_o.astype(BF16),
                ln1_g.reshape(1, D), ln1_b.reshape(1, D), rows=min(L, 512), alpha=alpha)

    d_ff = w_down.shape[0]
    cw = _ffn_chunk_width(d_ff)
    nch = d_ff // cw
    chunked = lambda a: jnp.concatenate(
        [a[..., :d_ff].reshape(a.shape[:-1] + (nch, cw)), a[..., d_ff:].reshape(a.shape[:-1] + (nch, cw))], axis=-1)
    wup = jnp.moveaxis(chunked(w_up), 1, 0).astype(BF16)
    cwt = jnp.moveaxis(chunked(conv_w), 1, 0)
    cwt = jnp.pad(cwt, ((0, 0), (0, SUBLANES - CONV_WIDTH), (0, 0)))
    cbs = jnp.moveaxis(chunked(conv_b[None, :]), 1, 0)
    wdn = w_down.reshape(nch, cw, D).astype(BF16)
    return _ffn(x1, wup, cwt, cbs, wdn, ln2_g.reshape(1, D), ln2_b.reshape(1, D), rows=min(L, 256), alpha=alpha)


def kernel(x, w_in, w_gla_a2, b_gla_a, gla_norm_gain, w_branch, w_o, ln1_gain, ln1_bias,
           w_up, conv_w, conv_b, w_down, ln2_gain, ln2_bias):
    depth = w_in.shape[0]
    for layer in range(depth):
        x = _layer(x, w_in[layer], w_gla_a2[layer], b_gla_a[layer], gla_norm_gain[layer], w_branch[layer],
                   w_o[layer], ln1_gain[layer], ln1_bias[layer], w_up[layer], conv_w[layer], conv_b[layer],
                   w_down[layer], ln2_gain[layer], ln2_bias[layer], depth)
    return x
```

```python
import functools
import math

import numpy as np
import jax
import jax.numpy as jnp
from jax import lax
from jax.experimental import pallas as pl
from jax.experimental.pallas import tpu as pltpu

F32 = jnp.float32
BF16 = jnp.bfloat16
I32 = jnp.int32

GLA_HEADS = 4
GLA_DK = 64
GLA_DV = 128
GLA_GATE_RANK = 16
GLA_TAU = 16.0
DSA_HEADS = 8
DSA_HEAD_DIM = 64
IDX_HEADS = 8
IDX_DIM = 32
DSA_TOPK_MAX = 256
ROPE_THETA = 500000.0
ROPE_FRACTION = 4
N_BRANCH = 2
BRANCH_WIDTH = 512
CONV_WIDTH = 3
LN_EPS = 1e-5
RMS_EPS = 1e-6

LANES = 128
SUBLANES = 8
VMEM_LIMIT_BYTES = 56 * 1024 * 1024

GLA_CHUNK = 64
GLA_SUB = 16
INT_MIN = -(2 ** 31)
NEG_BIG = -1e30
LOG2_E = 1.4426950408889634


def _cparams(sem):
    return pltpu.CompilerParams(dimension_semantics=sem, vmem_limit_bytes=VMEM_LIMIT_BYTES)


def _log2(n):
    assert n > 0 and n & (n - 1) == 0, f"{n} must be a power of two"
    return n.bit_length() - 1


def _div_pow2(x, n):
    return lax.shift_right_logical(x, jnp.int32(_log2(n)))


def _mod_pow2(x, n):
    _log2(n)
    return x & (n - 1)


def _dot(a, b):
    return jnp.dot(a, b, preferred_element_type=F32)


def _dot_nt(a, b):
    return lax.dot_general(a, b, (((1,), (1,)), ((), ())), preferred_element_type=F32)


def _dot_tn(a, b):
    return lax.dot_general(a, b, (((0,), (0,)), ((), ())), preferred_element_type=F32)


def _rope_table_kernel(invf_ref, c_ref, sa_ref, sb_ref, *, rows, periods, halves):
    t0 = pl.program_id(0) * rows
    pos = (t0 + lax.broadcasted_iota(I32, (rows, LANES), 0)).astype(F32)
    lane = lax.broadcasted_iota(I32, (rows, LANES), 1)
    for p, (period, half) in enumerate(zip(periods, halves)):
        ang = pos * invf_ref[p]
        c = jnp.cos(ang)
        s = jnp.sin(ang)
        m = _mod_pow2(lane, period)
        is1 = m < half
        is2 = (m >= half) & (m < 2 * half)
        c_ref[p] = jnp.where(is1 | is2, c, 1.0)
        sa_ref[p] = jnp.where(is1, -s, 0.0)
        sb_ref[p] = jnp.where(is2, s, 0.0)


def _rope_tables(L, invf):
    rows = min(L, 512)
    n = invf.shape[0]
    spec = pl.BlockSpec((n, rows, LANES), lambda i: (0, i, 0))
    shp = jax.ShapeDtypeStruct((n, L, LANES), F32)
    return pl.pallas_call(
        functools.partial(_rope_table_kernel, rows=rows, periods=(DSA_HEAD_DIM, IDX_DIM),
                          halves=(DSA_HEAD_DIM // ROPE_FRACTION // 2, IDX_DIM // ROPE_FRACTION // 2)),
        grid=(L // rows,),
        in_specs=[pl.BlockSpec((n, 1, LANES), lambda i: (0, 0, 0))],
        out_specs=[spec, spec, spec],
        out_shape=[shp, shp, shp],
        compiler_params=_cparams(("arbitrary",)),
        name="rope_tables",
    )(invf)


def _apply_rope(x, c, sa, sb, half, lane_limit=None):
    outs = []
    for g in range(x.shape[1] // LANES):
        xs = x[:, g * LANES:(g + 1) * LANES]
        up = pltpu.roll(xs, LANES - half, axis=1)
        dn = pltpu.roll(xs, half, axis=1)
        y = xs * c + up * sa + dn * sb
        if lane_limit is not None:
            lane = lax.broadcasted_iota(I32, xs.shape, 1)
            y = jnp.where(lane < lane_limit, y, xs)
        outs.append(y)
    return outs[0] if len(outs) == 1 else jnp.concatenate(outs, axis=1)


def _in_proj_kernel(x_ref, wgla_ref, wga_ref, wa2_ref, ba_ref, wdq_ref, wdkv_ref, wiq_ref, wik_ref, wiw_ref,
                    c_ref, sa_ref, sb_ref,
                    gla_ref, g_ref, dq_ref, dkv_ref, dkvt_ref, iq_ref, ik_ref, iw_ref):
    xb = x_ref[0].astype(BF16)
    hg = _dot(xb, wgla_ref[...])
    nqk = GLA_HEADS * GLA_DK
    gla_ref[0, :, 0:nqk] = (hg[:, 0:nqk] * (GLA_DK ** -0.5)).astype(BF16)
    gla_ref[0, :, nqk:] = hg[:, nqk:].astype(BF16)
    a_low = _dot(xb, wga_ref[...])
    z = _dot(a_low.astype(BF16), wa2_ref[...]) + ba_ref[...]
    g_ref[0] = (jnp.minimum(z, 0.0) - jnp.log1p(jnp.exp(-jnp.abs(z)))) * (1.0 / GLA_TAU)
    c64, sa64, sb64 = c_ref[0], sa_ref[0], sb_ref[0]
    c32, sa32, sb32 = c_ref[1], sa_ref[1], sb_ref[1]
    h64 = DSA_HEAD_DIM // ROPE_FRACTION // 2
    h32 = IDX_DIM // ROPE_FRACTION // 2
    dq = _dot(xb, wdq_ref[...])
    dq_ref[0] = (_apply_rope(dq, c64, sa64, sb64, h64) * (DSA_HEAD_DIM ** -0.5 * LOG2_E)).astype(BF16)
    dkv = _apply_rope(_dot(xb, wdkv_ref[...]), c64, sa64, sb64, h64, lane_limit=DSA_HEAD_DIM)
    dkv_ref[0] = dkv.astype(BF16)
    dkvt_ref[0, 0] = dkv.T.astype(BF16)
    iq = _dot(xb, wiq_ref[...])
    iq_ref[0] = _apply_rope(iq, c32, sa32, sb32, h32).astype(BF16)
    ik = _dot(xb, wik_ref[...])
    ik_ref[0] = _apply_rope(ik, c32, sa32, sb32, h32).astype(BF16)
    iw_ref[0] = _dot(xb, wiw_ref[...]) * ((IDX_HEADS ** -0.5) * (IDX_DIM ** -0.5))


def _in_proj(x, wts, tabs, rows):
    B, L, D = x.shape
    wgla, wga, wa2, ba, wdq, wdkv, wiq, wik, wiw = wts
    c, sa, sb = tabs
    n_l = L // rows

    def full(a):
        return pl.BlockSpec(a.shape, lambda i, b, _n=a.ndim: (0,) * _n)

    def rowblk(w):
        return pl.BlockSpec((1, rows, w), lambda i, b: (b, i, 0))

    tab = pl.BlockSpec((2, rows, LANES), lambda i, b: (0, i, 0))
    widths = (wgla.shape[1], GLA_HEADS * GLA_DK, wdq.shape[1], wdkv.shape[1], wiq.shape[1], wik.shape[1], wiw.shape[1])
    dtypes = (BF16, F32, BF16, BF16, BF16, BF16, F32)
    out_specs = [rowblk(w) for w in widths]
    out_shape = [jax.ShapeDtypeStruct((B, L, w), dt) for w, dt in zip(widths, dtypes)]
    out_specs.insert(4, pl.BlockSpec((1, 1, wdkv.shape[1], rows), lambda i, b: (b, i, 0, 0)))
    out_shape.insert(4, jax.ShapeDtypeStruct((B, n_l, wdkv.shape[1], rows), BF16))
    return pl.pallas_call(
        _in_proj_kernel,
        grid=(n_l, B),
        in_specs=[rowblk(D)] + [full(w) for w in wts] + [tab, tab, tab],
        out_specs=out_specs,
        out_shape=out_shape,
        compiler_params=_cparams(("arbitrary", "arbitrary")),
        name="in_proj",
    )(x, *wts, c, sa, sb)


def _gla_kernel(qkvr_ref, g_ref, gain_ref, tril_ref, expand_ref, y_ref, st_ref, *, rows):
    C, S = GLA_CHUNK, GLA_SUB
    H, DK, DV = GLA_HEADS, GLA_DK, GLA_DV
    HK, HV = H * DK, H * DV

    @pl.when(pl.program_id(1) == 0)
    def _():
        st_ref[...] = jnp.zeros_like(st_ref)

    ri = lax.broadcasted_iota(I32, (C, HK), 0)
    li = lax.broadcasted_iota(I32, (C, HK), 1)
    head_of_lane = _div_pow2(li, DK)
    sr = _div_pow2(lax.broadcasted_iota(I32, (HV, HK), 0), DV)
    scol = _div_pow2(lax.broadcasted_iota(I32, (HV, HK), 1), DK)
    st_mask = sr == scol
    tril = tril_ref[...]
    expand = expand_ref[...]
    gain = gain_ref[...]

    def chunk(ci, carry):
        r0 = pl.multiple_of(ci * C, C)
        blk = qkvr_ref[0, pl.ds(r0, C), :]
        q = blk[:, 0:HK].astype(F32)
        k = blk[:, HK:2 * HK].astype(F32)
        v = blk[:, 2 * HK:2 * HK + HV].astype(F32)
        r = blk[:, 2 * HK + HV:].astype(F32)
        g = g_ref[0, pl.ds(r0, C), :]
        g1 = g.astype(BF16)
        rem = g - g1.astype(F32)
        g2 = rem.astype(BF16)
        g3 = (rem - g2.astype(F32)).astype(BF16)
        b = _dot(tril, g1) + _dot(tril, g2) + _dot(tril, g3)
        b_last = b[C - 1:C, :]
        st = st_ref[...]
        o = _dot_nt((q * jnp.exp(b)).astype(BF16), st.astype(BF16))
        p_acc = jnp.zeros((H * C, C), F32)
        for I in range(1, C // S):
            beta = b[I * S - 1:I * S, :]
            in_I = (ri >= I * S) & (ri < (I + 1) * S)
            qI = jnp.where(in_I, q * jnp.exp(jnp.minimum(b - beta, 0.0)), 0.0)
            kI = jnp.where(ri < I * S, k * jnp.exp(jnp.minimum(beta - b, 0.0)), 0.0)
            q_st = jnp.concatenate([jnp.where(head_of_lane == h, qI, 0.0) for h in range(H)], axis=0)
            p_acc = p_acc + _dot_nt(q_st.astype(BF16), kI.astype(BF16))
        p_b = p_acc.astype(BF16)
        vb = v.astype(BF16)
        o_off = jnp.concatenate(
            [_dot(p_b[h * C:(h + 1) * C, :], vb[:, h * DV:(h + 1) * DV]) for h in range(H)], axis=1)
        o = o + o_off
        GRP = 4
        for d0 in range(0, S, GRP):
            ts = []
            for d in range(d0, d0 + GRP):
                if d == 0:
                    ts.append(q * k)
                else:
                    kd = pltpu.roll(k, d, axis=0)
                    bd = pltpu.roll(b, d, axis=0)
                    valid = _mod_pow2(ri, S) >= d
                    ts.append(q * kd * jnp.exp(jnp.where(valid, jnp.minimum(b - bd, 0.0), -jnp.inf)))
            s_all = _dot(jnp.concatenate(ts, axis=0).astype(BF16), expand)
            for n, d in enumerate(range(d0, d0 + GRP)):
                vd = v if d == 0 else pltpu.roll(v, d, axis=0)
                o = o + s_all[n * C:(n + 1) * C, :] * vd
        k_end = k * jnp.exp(b_last - b)
        upd = _dot_tn(vb, k_end.astype(BF16))
        st_ref[...] = st * jnp.exp(b_last) + jnp.where(st_mask, upd, 0.0)
        outs = []
        for h in range(H):
            oh = o[:, h * DV:(h + 1) * DV]
            ms = jnp.mean(oh * oh, axis=-1, keepdims=True)
            outs.append(oh * lax.rsqrt(ms + RMS_EPS))
        on = jnp.concatenate(outs, axis=1) * gain
        y = on * (r * jax.nn.sigmoid(r))
        y_ref[0, pl.ds(r0, C), :] = y.astype(y_ref.dtype)
        return carry

    lax.fori_loop(0, rows // C, chunk, 0)


def _gla(qkvr, g, gain, rows):
    B, L, W = qkvr.shape
    HK, HV = GLA_HEADS * GLA_DK, GLA_HEADS * GLA_DV
    C = GLA_CHUNK
    tril = jnp.asarray(np.tril(np.ones((C, C), np.float32)), BF16)
    expand = jnp.asarray(np.kron(np.eye(GLA_HEADS, dtype=np.float32), np.ones((GLA_DK, GLA_DV), np.float32)), BF16)
    return pl.pallas_call(
        functools.partial(_gla_kernel, rows=rows),
        grid=(B, L // rows),
        in_specs=[pl.BlockSpec((1, rows, W), lambda b, i: (b, i, 0)),
                  pl.BlockSpec((1, rows, HK), lambda b, i: (b, i, 0)),
                  pl.BlockSpec((1, HV), lambda b, i: (0, 0)),
                  pl.BlockSpec((C, C), lambda b, i: (0, 0)),
                  pl.BlockSpec((HK, HV), lambda b, i: (0, 0))],
        out_specs=pl.BlockSpec((1, rows, HV), lambda b, i: (b, i, 0)),
        out_shape=jax.ShapeDtypeStruct((B, L, HV), BF16),
        scratch_shapes=[pltpu.VMEM((HV, HK), F32)],
        compiler_params=_cparams(("arbitrary", "arbitrary")),
        name="gla",
    )(qkvr, g, gain, tril, expand)


def _dsa_kernel(dq_ref, kv_ref, kvt_ref, iq_ref, ik_ref, iw_ref, o_ref,
                key_ref, qt_ref, qit_ref, wt_ref, m_ref, l_ref, acc_ref, s_ref, p_ref, *, tq, tk, top_k):
    NH, DH = DSA_HEADS, DSA_HEAD_DIM
    assert tq == LANES
    i = pl.program_id(1)
    t0 = i * tq
    nkb = (t0 + tq + tk - 1) // tk

    qt_ref[DH:, :] = jnp.zeros((DH, NH * tq), BF16)
    for p in range(NH // 2):
        pair_t = dq_ref[0, :, p * LANES:(p + 1) * LANES].astype(F32).T
        qt_ref[0:DH, (2 * p) * tq:(2 * p + 1) * tq] = pair_t[0:DH].astype(BF16)
        qt_ref[0:DH, (2 * p + 1) * tq:(2 * p + 2) * tq] = pair_t[DH:].astype(BF16)
    qit_ref[...] = jnp.zeros(qit_ref.shape, BF16)
    heads_per_group = LANES // IDX_DIM
    for g in range(IDX_HEADS // heads_per_group):
        grp_t = iq_ref[0, :, g * LANES:(g + 1) * LANES].astype(F32).T
        for u in range(heads_per_group):
            h = g * heads_per_group + u
            qit_ref[0:IDX_DIM, h * tq:(h + 1) * tq] = grp_t[u * IDX_DIM:(u + 1) * IDX_DIM].astype(BF16)
    wt_ref[...] = iw_ref[0].T[0:SUBLANES]

    q_pos = t0 + lax.broadcasted_iota(I32, (tk, tq), 1)
    k_off = lax.broadcasted_iota(I32, (tk, tq), 0)

    def score_block(j, carry):
        k0 = pl.multiple_of(j * tk, tk)
        ikb = ik_ref[0, pl.ds(k0, tk), :]
        sc = jnp.zeros((tk, tq), F32)
        for c in range(IDX_HEADS // 2):
            lg = _dot(ikb, qit_ref[:, (2 * c) * tq:(2 * c + 2) * tq])
            for u in range(2):
                h = 2 * c + u
                sc = sc + jnp.maximum(lg[:, u * tq:(u + 1) * tq], 0.0) * wt_ref[h:h + 1, :]
        sc = sc + 0.0
        bits = pltpu.bitcast(sc, I32)
        key = bits ^ ((bits >> 31) & 0x7FFFFFFF)
        key_ref[j] = jnp.where(k0 + k_off <= q_pos, key, INT_MIN)
        return carry

    lax.fori_loop(0, nkb, score_block, 0)

    def count_rows(hit_fn):
        acc_rows = 8 * SUBLANES

        def body(j, cnt):
            hit = hit_fn(j)
            return cnt + jnp.sum(hit.reshape(tk // acc_rows, acc_rows, tq), axis=0)

        cnt = lax.fori_loop(0, nkb, body, jnp.zeros((acc_rows, tq), F32))
        return jnp.sum(cnt, axis=0, keepdims=True)

    def count_ge(cand):
        return count_rows(lambda j: jnp.where(key_ref[j] >= cand, 1.0, 0.0))

    def search_bit(it, thr):
        cand = thr + lax.shift_left(jnp.int32(1), 31 - it)
        return jnp.where(count_ge(cand) >= top_k, cand, thr)

    thr = lax.fori_loop(0, 32, search_bit, jnp.full((1, tq), INT_MIN, I32))
    thr = jnp.maximum(thr, INT_MIN + 1)
    c_ge = count_ge(thr)
    c_gt = count_ge(thr + 1)
    excess = jnp.max(jnp.where(c_ge > top_k, 1.0, 0.0))

    @pl.when(excess > 0.0)
    def _():
        need = top_k - c_gt
        nbits = max(1, int(math.ceil(math.log2(key_ref.shape[0] * tk))))

        def count_eq_le(idx_lim):
            return count_rows(
                lambda j: jnp.where((key_ref[j] == thr) & (j * tk + k_off <= idx_lim), 1.0, 0.0))

        def idx_bit(it, lo):
            trial = lo + lax.shift_left(jnp.int32(1), nbits - 1 - it)
            return jnp.where(count_eq_le(trial - 1) < need, trial, lo)

        cut = lax.fori_loop(0, nbits, idx_bit, jnp.zeros((1, tq), I32))

        def drop(j, carry):
            key = key_ref[j]
            key_ref[j] = jnp.where((key == thr) & (j * tk + k_off > cut), INT_MIN, key)
            return carry

        lax.fori_loop(0, nkb, drop, 0)

    m_ref[...] = jnp.full(m_ref.shape, NEG_BIG, F32)
    l_ref[...] = jnp.zeros(l_ref.shape, F32)
    acc_ref[...] = jnp.zeros(acc_ref.shape, F32)
    cw = 2 * tq
    n_col = NH * tq
    rc = 4 * SUBLANES

    def attend(j, carry):
        k0 = pl.multiple_of(j * tk, tk)
        kvb = kv_ref[0, pl.ds(k0, tk), :]
        bias = jnp.where(key_ref[j] >= thr, 0.0, NEG_BIG)
        bias2 = jnp.concatenate([bias, bias], axis=1)
        maxes = []
        for c in range(NH // 2):
            cols = slice(c * cw, (c + 1) * cw)
            s = _dot(kvb, qt_ref[:, cols]) + bias2
            s_ref[:, cols] = s
            maxes.append(jnp.max(s, axis=0, keepdims=True))
        m_old = m_ref[...]
        m_new = jnp.maximum(m_old, jnp.concatenate(maxes, axis=1))
        alpha = jnp.exp2(m_old - m_new)
        m_ref[...] = m_new

        def exp_rows(r, psum):
            r0 = pl.multiple_of(r * rc, rc)
            p = jnp.exp2(s_ref[pl.ds(r0, rc), :] - m_new)
            p_ref[pl.ds(r0, rc), :] = p.astype(BF16)
            return psum + jnp.sum(p.reshape(rc // SUBLANES, SUBLANES, n_col), axis=0)

        psum = lax.fori_loop(0, tk // rc, exp_rows, jnp.zeros((SUBLANES, n_col), F32), unroll=2)
        l_ref[...] = l_ref[...] * alpha + jnp.sum(psum, axis=0, keepdims=True)
        vt = kvt_ref[0, j, DH:, :]
        for c in range(NH // 2):
            cols = slice(c * cw, (c + 1) * cw)
            acc_ref[:, cols] = acc_ref[:, cols] * alpha[:, cols] + _dot(vt, p_ref[:, cols])
        return carry

    lax.fori_loop(0, nkb, attend, 0)

    for p in range(NH // 2):
        pair_t = jnp.concatenate(
            [acc_ref[:, (2 * p + u) * tq:(2 * p + u + 1) * tq] / l_ref[:, (2 * p + u) * tq:(2 * p + u + 1) * tq]
             for u in range(2)], axis=0)
        o_ref[0, :, p * LANES:(p + 1) * LANES] = pair_t.T.astype(o_ref.dtype)


def _dsa(dq, dkv, dkvt, iq, ik, iw, tq, tk):
    B, L, _ = dq.shape
    top_k = min(DSA_TOPK_MAX, L // 4)
    nkb_max = L // tk
    assert dkvt.shape == (B, nkb_max, dkv.shape[2], tk)
    return pl.pallas_call(
        functools.partial(_dsa_kernel, tq=tq, tk=tk, top_k=float(top_k)),
        grid=(B, L // tq),
        in_specs=[pl.BlockSpec((1, tq, dq.shape[2]), lambda b, i: (b, i, 0)),
                  pl.BlockSpec((1, L, dkv.shape[2]), lambda b, i: (b, 0, 0)),
                  pl.BlockSpec((1,) + dkvt.shape[1:], lambda b, i: (b, 0, 0, 0)),
                  pl.BlockSpec((1, tq, iq.shape[2]), lambda b, i: (b, i, 0)),
                  pl.BlockSpec((1, L, ik.shape[2]), lambda b, i: (b, 0, 0)),
                  pl.BlockSpec((1, tq, iw.shape[2]), lambda b, i: (b, i, 0))],
        out_specs=pl.BlockSpec((1, tq, dq.shape[2]), lambda b, i: (b, i, 0)),
        out_shape=jax.ShapeDtypeStruct(dq.shape, BF16),
        scratch_shapes=[pltpu.VMEM((nkb_max, tk, tq), I32),
                        pltpu.VMEM((2 * DSA_HEAD_DIM, DSA_HEADS * tq), BF16),
                        pltpu.VMEM((LANES, IDX_HEADS * tq), BF16),
                        pltpu.VMEM((SUBLANES, tq), F32),
                        pltpu.VMEM((1, DSA_HEADS * tq), F32),
                        pltpu.VMEM((1, DSA_HEADS * tq), F32),
                        pltpu.VMEM((DSA_HEAD_DIM, DSA_HEADS * tq), F32),
                        pltpu.VMEM((tk, DSA_HEADS * tq), F32),
                        pltpu.VMEM((tk, DSA_HEADS * tq), BF16)],
        compiler_params=_cparams(("arbitrary", "arbitrary")),
        name="dsa",
    )(dq, dkv, dkvt, iq, ik, iw)


def _layer_norm(z, gain, bias):
    mu = jnp.mean(z, axis=-1, keepdims=True)
    zc = z - mu
    var = jnp.mean(zc * zc, axis=-1, keepdims=True)
    return zc * lax.rsqrt(var + LN_EPS) * gain + bias


def _merge_kernel(x_ref, yg_ref, yd_ref, wgate_ref, wbr_ref, wo_ref, gain_ref, bias_ref, o_ref, *, alpha):
    x = x_ref[0]
    D = x.shape[1]
    gates = jax.nn.sigmoid(_dot(x.astype(BF16), wgate_ref[...]))
    pg = _dot(yg_ref[0], wbr_ref[0])
    pd = _dot(yd_ref[0], wbr_ref[1])
    m = gates[:, 0:D] * pg + gates[:, D:] * pd
    mixed = _dot(m.astype(BF16), wo_ref[...])
    o_ref[0] = _layer_norm(alpha * x + mixed, gain_ref[...], bias_ref[...])


def _merge(x, yg, yd, wgate, wbr, wo, gain, bias, rows, alpha):
    B, L, D = x.shape

    def full(a):
        return pl.BlockSpec(a.shape, lambda b, i, _n=a.ndim: (0,) * _n)

    def rowblk(w):
        return pl.BlockSpec((1, rows, w), lambda b, i: (b, i, 0))

    return pl.pallas_call(
        functools.partial(_merge_kernel, alpha=alpha),
        grid=(B, L // rows),
        in_specs=[rowblk(D), rowblk(yg.shape[2]), rowblk(yd.shape[2]),
                  full(wgate), full(wbr), full(wo), full(gain), full(bias)],
        out_specs=rowblk(D),
        out_shape=jax.ShapeDtypeStruct((B, L, D), F32),
        compiler_params=_cparams(("arbitrary", "arbitrary")),
        name="merge_ln1",
    )(x, yg, yd, wgate, wbr, wo, gain, bias)


def _ffn_kernel(x_ref, wup_ref, cw_ref, cb_ref, wdn_ref, gain_ref, bias_ref, o_ref, hist_ref, *, alpha, n_chunks, cw):
    rows = x_ref.shape[1]

    @pl.when(pl.program_id(1) == 0)
    def _():
        hist_ref[...] = jnp.zeros_like(hist_ref)

    x = x_ref[0]
    xb = x.astype(BF16)
    ri = lax.broadcasted_iota(I32, (rows, 2 * cw), 0)

    def chunk(c, f):
        u = _dot(xb, wup_ref[c])
        hist = hist_ref[c]
        h1 = jnp.broadcast_to(hist[SUBLANES - 1:SUBLANES, :], u.shape)
        h2 = jnp.broadcast_to(hist[SUBLANES - 2:SUBLANES - 1, :], u.shape)
        u1 = jnp.where(ri == 0, h1, pltpu.roll(u, 1, axis=0))
        u2 = jnp.where(ri == 0, h2, jnp.where(ri == 1, h1, pltpu.roll(u, 2, axis=0)))
        hist_ref[c] = u[rows - SUBLANES:, :]
        w = cw_ref[c]
        cu = cb_ref[c] + u2 * w[0:1, :] + u1 * w[1:2, :] + u * w[2:3, :]
        ug = cu[:, 0:cw]
        act = (ug * jax.nn.sigmoid(ug)) * cu[:, cw:]
        return f + _dot(act.astype(BF16), wdn_ref[c])

    f = lax.fori_loop(0, n_chunks, chunk, jnp.zeros(x.shape, F32))
    o_ref[0] = _layer_norm(alpha * x + f, gain_ref[...], bias_ref[...])


def _ffn(x, wup, cwt, cbs, wdn, gain, bias, rows, alpha):
    B, L, D = x.shape
    n_chunks, _, cw2 = wup.shape

    def full(a):
        return pl.BlockSpec(a.shape, lambda b, i, _n=a.ndim: (0,) * _n)

    rowblk = pl.BlockSpec((1, rows, D), lambda b, i: (b, i, 0))
    return pl.pallas_call(
        functools.partial(_ffn_kernel, alpha=alpha, n_chunks=n_chunks, cw=cw2 // 2),
        grid=(B, L // rows),
        in_specs=[rowblk, full(wup), full(cwt), full(cbs), full(wdn), full(gain), full(bias)],
        out_specs=rowblk,
        out_shape=jax.ShapeDtypeStruct((B, L, D), F32),
        scratch_shapes=[pltpu.VMEM((n_chunks, SUBLANES, cw2), F32)],
        compiler_params=_cparams(("arbitrary", "arbitrary")),
        name="ffn_ln2",
    )(x, wup, cwt, cbs, wdn, gain, bias)


def _pad_cols(w, width):
    return jnp.pad(w, ((0, 0), (0, width - w.shape[1])))


def _ffn_chunk_width(d_ff):
    for cw in (512, 384, 256, 128):
        if d_ff % cw == 0:
            return cw
    raise ValueError(f"d_ff={d_ff} must be a multiple of {LANES}")


def _token_mixers(x, w_in, w_a2, b_a, gla_gain):
    B, L, D = x.shape
    nqk = GLA_HEADS * GLA_DK
    nv = GLA_HEADS * GLA_DV
    ndq = DSA_HEADS * DSA_HEAD_DIM
    niq = IDX_HEADS * IDX_DIM
    widths = (nqk, nqk, nv, GLA_GATE_RANK, nv, ndq, DSA_HEAD_DIM, DSA_HEAD_DIM, niq, IDX_DIM, IDX_HEADS, N_BRANCH * D)
    offs = np.concatenate([[0], np.cumsum(widths)])
    assert w_in.shape == (D, int(offs[-1]))
    col = lambda n: w_in[:, int(offs[n]):int(offs[n + 1])]
    w_gq, w_gk, w_gv, w_ga, w_gr, w_dq, w_dk, w_dv, w_iq, w_ik, w_iw, w_gate = [col(n) for n in range(12)]
    wts = (
        jnp.concatenate([w_gq, w_gk, w_gv, w_gr], axis=1).astype(BF16),
        _pad_cols(w_ga, LANES).astype(BF16),
        jnp.pad(w_a2, ((0, LANES - GLA_GATE_RANK), (0, 0))).astype(BF16),
        b_a.reshape(1, nqk).astype(F32),
        w_dq.astype(BF16),
        jnp.concatenate([w_dk, w_dv], axis=1).astype(BF16),
        w_iq.astype(BF16),
        _pad_cols(w_ik, LANES).astype(BF16),
        _pad_cols(w_iw, LANES).astype(BF16),
    )
    def lane_freqs(head_dim):
        rot = head_dim // ROPE_FRACTION
        half = rot // 2
        inv = ROPE_THETA ** (-jnp.arange(half, dtype=F32) * 2.0 / rot)
        m = np.arange(LANES) % head_dim
        return jnp.where(jnp.asarray(m < rot), inv[jnp.asarray(m % half)], 0.0)

    invf = jnp.stack([lane_freqs(DSA_HEAD_DIM), lane_freqs(IDX_DIM)]).reshape(2, 1, LANES)
    tabs = _rope_tables(L, invf)

    tq = LANES
    tk = min(L, 512)
    gla_in, g, dq, dkv, dkvt, iq, ik, iw = _in_proj(x, wts, tabs, rows=tk)
    y_gla = _gla(gla_in, g, gla_gain.reshape(1, nv).astype(F32), rows=min(L, 512))
    y_dsa = _dsa(dq, dkv, dkvt, iq, ik, iw, tq, tk)
    return y_gla, y_dsa, w_gate


def _layer(x, w_in, w_a2, b_a, gla_gain, w_branch, w_o, ln1_g, ln1_b, w_up, conv_w, conv_b, w_down, ln2_g, ln2_b,
           depth):
    B, L, D = x.shape
    alpha = (2.0 * depth) ** 0.25
    y_gla, y_dsa, w_gate = _token_mixers(x, w_in, w_a2, b_a, gla_gain)
    x1 = _merge(x, y_gla, y_dsa, w_gate.astype(BF16), w_branch.astype(BF16), w_o.astype(BF16),
                ln1_g.reshape(1, D), ln1_b.reshape(1, D), rows=min(L, 512), alpha=alpha)

    d_ff = w_down.shape[0]
    cw = _ffn_chunk_width(d_ff)
    nch = d_ff // cw
    chunked = lambda a: jnp.concatenate(
        [a[..., :d_ff].reshape(a.shape[:-1] + (nch, cw)), a[..., d_ff:].reshape(a.shape[:-1] + (nch, cw))], axis=-1)
    wup = jnp.moveaxis(chunked(w_up), 1, 0).astype(BF16)
    cwt = jnp.moveaxis(chunked(conv_w), 1, 0)
    cwt = jnp.pad(cwt, ((0, 0), (0, SUBLANES - CONV_WIDTH), (0, 0)))
    cbs = jnp.moveaxis(chunked(conv_b[None, :]), 1, 0)
    wdn = w_down.reshape(nch, cw, D).astype(BF16)
    return _ffn(x1, wup, cwt, cbs, wdn, ln2_g.reshape(1, D), ln2_b.reshape(1, D), rows=min(L, 256), alpha=alpha)


def kernel(x, w_in, w_gla_a2, b_gla_a, gla_norm_gain, w_branch, w_o, ln1_gain, ln1_bias,
           w_up, conv_w, conv_b, w_down, ln2_gain, ln2_bias):
    depth = w_in.shape[0]
    for layer in range(depth):
        x = _layer(x, w_in[layer], w_gla_a2[layer], b_gla_a[layer], gla_norm_gain[layer], w_branch[layer],
                   w_o[layer], ln1_gain[layer], ln1_bias[layer], w_up[layer], conv_w[layer], conv_b[layer],
                   w_down[layer], ln2_gain[layer], ln2_bias[layer], depth)
    return x
```

```python
import functools
import math

import numpy as np
import jax
import jax.numpy as jnp
from jax import lax
from jax.experimental import pallas as pl
from jax.experimental.pallas import tpu as pltpu

F32 = jnp.float32
BF16 = jnp.bfloat16
I32 = jnp.int32
I16 = jnp.int16

GLA_HEADS = 4
GLA_DK = 64
GLA_DV = 128
GLA_GATE_RANK = 16
GLA_TAU = 16.0
DSA_HEADS = 8
DSA_HEAD_DIM = 64
IDX_HEADS = 8
IDX_DIM = 32
DSA_TOPK_MAX = 256
ROPE_THETA = 500000.0
ROPE_FRACTION = 4
N_BRANCH = 2
BRANCH_WIDTH = 512
CONV_WIDTH = 3
LN_EPS = 1e-5
RMS_EPS = 1e-6

LANES = 128
SUBLANES = 8
VMEM_LIMIT_BYTES = 56 * 1024 * 1024

GLA_CHUNK = 64
GLA_SUB = 16
I16_ROWS = 2 * SUBLANES
INT_MIN = -(2 ** 31)
I16_MIN = -(2 ** 15)
I16_MAX = 2 ** 15 - 1
NEG_BIG = -1e30
LOG2_E = 1.4426950408889634


def _cparams(sem):
    return pltpu.CompilerParams(dimension_semantics=sem, vmem_limit_bytes=VMEM_LIMIT_BYTES)


def _log2(n):
    assert n > 0 and n & (n - 1) == 0, f"{n} must be a power of two"
    return n.bit_length() - 1


def _div_pow2(x, n):
    return lax.shift_right_logical(x, jnp.int32(_log2(n)))


def _mod_pow2(x, n):
    _log2(n)
    return x & (n - 1)


def _dot(a, b):
    return jnp.dot(a, b, preferred_element_type=F32)


def _dot_nt(a, b):
    return lax.dot_general(a, b, (((1,), (1,)), ((), ())), preferred_element_type=F32)


def _dot_tn(a, b):
    return lax.dot_general(a, b, (((0,), (0,)), ((), ())), preferred_element_type=F32)


def _rope_table_kernel(invf_ref, c_ref, sa_ref, sb_ref, *, rows, periods, halves):
    t0 = pl.program_id(0) * rows
    pos = (t0 + lax.broadcasted_iota(I32, (rows, LANES), 0)).astype(F32)
    lane = lax.broadcasted_iota(I32, (rows, LANES), 1)
    for p, (period, half) in enumerate(zip(periods, halves)):
        ang = pos * invf_ref[p]
        c = jnp.cos(ang)
        s = jnp.sin(ang)
        m = _mod_pow2(lane, period)
        is1 = m < half
        is2 = (m >= half) & (m < 2 * half)
        c_ref[p] = jnp.where(is1 | is2, c, 1.0)
        sa_ref[p] = jnp.where(is1, -s, 0.0)
        sb_ref[p] = jnp.where(is2, s, 0.0)


def _rope_tables(L, invf):
    rows = min(L, 512)
    n = invf.shape[0]
    spec = pl.BlockSpec((n, rows, LANES), lambda i: (0, i, 0))
    shp = jax.ShapeDtypeStruct((n, L, LANES), F32)
    return pl.pallas_call(
        functools.partial(_rope_table_kernel, rows=rows, periods=(DSA_HEAD_DIM, IDX_DIM),
                          halves=(DSA_HEAD_DIM // ROPE_FRACTION // 2, IDX_DIM // ROPE_FRACTION // 2)),
        grid=(L // rows,),
        in_specs=[pl.BlockSpec((n, 1, LANES), lambda i: (0, 0, 0))],
        out_specs=[spec, spec, spec],
        out_shape=[shp, shp, shp],
        compiler_params=_cparams(("arbitrary",)),
        name="rope_tables",
    )(invf)


def _apply_rope(x, c, sa, sb, half, lane_limit=None):
    outs = []
    for g in range(x.shape[1] // LANES):
        xs = x[:, g * LANES:(g + 1) * LANES]
        up = pltpu.roll(xs, LANES - half, axis=1)
        dn = pltpu.roll(xs, half, axis=1)
        y = xs * c + up * sa + dn * sb
        if lane_limit is not None:
            lane = lax.broadcasted_iota(I32, xs.shape, 1)
            y = jnp.where(lane < lane_limit, y, xs)
        outs.append(y)
    return outs[0] if len(outs) == 1 else jnp.concatenate(outs, axis=1)


def _in_proj_kernel(x_ref, wgla_ref, wga_ref, wa2_ref, ba_ref, wdq_ref, wdkv_ref, wiq_ref, wik_ref, wiw_ref,
                    c_ref, sa_ref, sb_ref,
                    gla_ref, g_ref, dq_ref, dkv_ref, dkvt_ref, iq_ref, ik_ref, iw_ref):
    xb = x_ref[0].astype(BF16)
    hg = _dot(xb, wgla_ref[...])
    nqk = GLA_HEADS * GLA_DK
    gla_ref[0, :, 0:nqk] = (hg[:, 0:nqk] * (GLA_DK ** -0.5)).astype(BF16)
    gla_ref[0, :, nqk:] = hg[:, nqk:].astype(BF16)
    a_low = _dot(xb, wga_ref[...])
    z = _dot(a_low.astype(BF16), wa2_ref[...]) + ba_ref[...]
    g_ref[0] = (jnp.minimum(z, 0.0) - jnp.log1p(jnp.exp(-jnp.abs(z)))) * (1.0 / GLA_TAU)
    c64, sa64, sb64 = c_ref[0], sa_ref[0], sb_ref[0]
    c32, sa32, sb32 = c_ref[1], sa_ref[1], sb_ref[1]
    h64 = DSA_HEAD_DIM // ROPE_FRACTION // 2
    h32 = IDX_DIM // ROPE_FRACTION // 2
    dq = _dot(xb, wdq_ref[...])
    dq_ref[0] = (_apply_rope(dq, c64, sa64, sb64, h64) * (DSA_HEAD_DIM ** -0.5 * LOG2_E)).astype(BF16)
    dkv = _apply_rope(_dot(xb, wdkv_ref[...]), c64, sa64, sb64, h64, lane_limit=DSA_HEAD_DIM)
    dkv_ref[0] = dkv.astype(BF16)
    dkvt_ref[0, 0] = dkv.T.astype(BF16)
    iq = _dot(xb, wiq_ref[...])
    iq_ref[0] = _apply_rope(iq, c32, sa32, sb32, h32).astype(BF16)
    ik = _dot(xb, wik_ref[...])
    ik_ref[0] = _apply_rope(ik, c32, sa32, sb32, h32).astype(BF16)
    iw_ref[0] = _dot(xb, wiw_ref[...]) * ((IDX_HEADS ** -0.5) * (IDX_DIM ** -0.5))


def _in_proj(x, wts, tabs, rows):
    B, L, D = x.shape
    wgla, wga, wa2, ba, wdq, wdkv, wiq, wik, wiw = wts
    c, sa, sb = tabs
    n_l = L // rows

    def full(a):
        return pl.BlockSpec(a.shape, lambda i, b, _n=a.ndim: (0,) * _n)

    def rowblk(w):
        return pl.BlockSpec((1, rows, w), lambda i, b: (b, i, 0))

    tab = pl.BlockSpec((2, rows, LANES), lambda i, b: (0, i, 0))
    widths = (wgla.shape[1], GLA_HEADS * GLA_DK, wdq.shape[1], wdkv.shape[1], wiq.shape[1], wik.shape[1], wiw.shape[1])
    dtypes = (BF16, F32, BF16, BF16, BF16, BF16, F32)
    out_specs = [rowblk(w) for w in widths]
    out_shape = [jax.ShapeDtypeStruct((B, L, w), dt) for w, dt in zip(widths, dtypes)]
    out_specs.insert(4, pl.BlockSpec((1, 1, wdkv.shape[1], rows), lambda i, b: (b, i, 0, 0)))
    out_shape.insert(4, jax.ShapeDtypeStruct((B, n_l, wdkv.shape[1], rows), BF16))
    return pl.pallas_call(
        _in_proj_kernel,
        grid=(n_l, B),
        in_specs=[rowblk(D)] + [full(w) for w in wts] + [tab, tab, tab],
        out_specs=out_specs,
        out_shape=out_shape,
        compiler_params=_cparams(("arbitrary", "arbitrary")),
        name="in_proj",
    )(x, *wts, c, sa, sb)


def _gla_kernel(qkvr_ref, g_ref, gain_ref, tril_ref, expand_ref, y_ref, st_ref, *, rows):
    C, S = GLA_CHUNK, GLA_SUB
    H, DK, DV = GLA_HEADS, GLA_DK, GLA_DV
    HK, HV = H * DK, H * DV

    @pl.when(pl.program_id(1) == 0)
    def _():
        st_ref[...] = jnp.zeros_like(st_ref)

    ri = lax.broadcasted_iota(I32, (C, HK), 0)
    li = lax.broadcasted_iota(I32, (C, HK), 1)
    head_of_lane = _div_pow2(li, DK)
    sr = _div_pow2(lax.broadcasted_iota(I32, (HV, HK), 0), DV)
    scol = _div_pow2(lax.broadcasted_iota(I32, (HV, HK), 1), DK)
    st_mask = sr == scol
    tril = tril_ref[...]
    expand = expand_ref[...]
    gain = gain_ref[...]

    def chunk(ci, carry):
        r0 = pl.multiple_of(ci * C, C)
        blk = qkvr_ref[0, pl.ds(r0, C), :]
        q = blk[:, 0:HK].astype(F32)
        k = blk[:, HK:2 * HK].astype(F32)
        v = blk[:, 2 * HK:2 * HK + HV].astype(F32)
        r = blk[:, 2 * HK + HV:].astype(F32)
        g = g_ref[0, pl.ds(r0, C), :]
        g1 = g.astype(BF16)
        rem = g - g1.astype(F32)
        g2 = rem.astype(BF16)
        g3 = (rem - g2.astype(F32)).astype(BF16)
        b = _dot(tril, g1) + _dot(tril, g2) + _dot(tril, g3)
        b_last = b[C - 1:C, :]
        st = st_ref[...]
        o = _dot_nt((q * jnp.exp(b)).astype(BF16), st.astype(BF16))
        p_acc = jnp.zeros((H * C, C), F32)
        for I in range(1, C // S):
            beta = b[I * S - 1:I * S, :]
            in_I = (ri >= I * S) & (ri < (I + 1) * S)
            qI = jnp.where(in_I, q * jnp.exp(jnp.minimum(b - beta, 0.0)), 0.0)
            kI = jnp.where(ri < I * S, k * jnp.exp(jnp.minimum(beta - b, 0.0)), 0.0)
            q_st = jnp.concatenate([jnp.where(head_of_lane == h, qI, 0.0) for h in range(H)], axis=0)
            p_acc = p_acc + _dot_nt(q_st.astype(BF16), kI.astype(BF16))
        p_b = p_acc.astype(BF16)
        vb = v.astype(BF16)
        o_off = jnp.concatenate(
            [_dot(p_b[h * C:(h + 1) * C, :], vb[:, h * DV:(h + 1) * DV]) for h in range(H)], axis=1)
        o = o + o_off
        GRP = 4
        for d0 in range(0, S, GRP):
            ts = []
            for d in range(d0, d0 + GRP):
                if d == 0:
                    ts.append(q * k)
                else:
                    kd = pltpu.roll(k, d, axis=0)
                    bd = pltpu.roll(b, d, axis=0)
                    valid = _mod_pow2(ri, S) >= d
                    ts.append(q * kd * jnp.exp(jnp.where(valid, jnp.minimum(b - bd, 0.0), -jnp.inf)))
            s_all = _dot(jnp.concatenate(ts, axis=0).astype(BF16), expand)
            for n, d in enumerate(range(d0, d0 + GRP)):
                vd = v if d == 0 else pltpu.roll(v, d, axis=0)
                o = o + s_all[n * C:(n + 1) * C, :] * vd
        k_end = k * jnp.exp(b_last - b)
        upd = _dot_tn(vb, k_end.astype(BF16))
        st_ref[...] = st * jnp.exp(b_last) + jnp.where(st_mask, upd, 0.0)
        outs = []
        for h in range(H):
            oh = o[:, h * DV:(h + 1) * DV]
            ms = jnp.mean(oh * oh, axis=-1, keepdims=True)
            outs.append(oh * lax.rsqrt(ms + RMS_EPS))
        on = jnp.concatenate(outs, axis=1) * gain
        y = on * (r * jax.nn.sigmoid(r))
        y_ref[0, pl.ds(r0, C), :] = y.astype(y_ref.dtype)
        return carry

    lax.fori_loop(0, rows // C, chunk, 0)


def _gla(qkvr, g, gain, rows):
    B, L, W = qkvr.shape
    HK, HV = GLA_HEADS * GLA_DK, GLA_HEADS * GLA_DV
    C = GLA_CHUNK
    tril = jnp.asarray(np.tril(np.ones((C, C), np.float32)), BF16)
    expand = jnp.asarray(np.kron(np.eye(GLA_HEADS, dtype=np.float32), np.ones((GLA_DK, GLA_DV), np.float32)), BF16)
    return pl.pallas_call(
        functools.partial(_gla_kernel, rows=rows),
        grid=(B, L // rows),
        in_specs=[pl.BlockSpec((1, rows, W), lambda b, i: (b, i, 0)),
                  pl.BlockSpec((1, rows, HK), lambda b, i: (b, i, 0)),
                  pl.BlockSpec((1, HV), lambda b, i: (0, 0)),
                  pl.BlockSpec((C, C), lambda b, i: (0, 0)),
                  pl.BlockSpec((HK, HV), lambda b, i: (0, 0))],
        out_specs=pl.BlockSpec((1, rows, HV), lambda b, i: (b, i, 0)),
        out_shape=jax.ShapeDtypeStruct((B, L, HV), BF16),
        scratch_shapes=[pltpu.VMEM((HV, HK), F32)],
        compiler_params=_cparams(("arbitrary", "arbitrary")),
        name="gla",
    )(qkvr, g, gain, tril, expand)


def _dsa_kernel(dq_ref, kv_ref, kvt_ref, iq_ref, ik_ref, iw_ref, o_ref,
                key_ref, half_ref, qt_ref, qit_ref, wt_ref, m_ref, l_ref, acc_ref, s_ref, p_ref, *, tq, tk, top_k):
    NH, DH = DSA_HEADS, DSA_HEAD_DIM
    assert tq == LANES
    i = pl.program_id(1)
    t0 = i * tq
    nkb = (t0 + tq + tk - 1) // tk

    qt_ref[DH:, :] = jnp.zeros((DH, NH * tq), BF16)
    for p in range(NH // 2):
        pair_t = dq_ref[0, :, p * LANES:(p + 1) * LANES].astype(F32).T
        qt_ref[0:DH, (2 * p) * tq:(2 * p + 1) * tq] = pair_t[0:DH].astype(BF16)
        qt_ref[0:DH, (2 * p + 1) * tq:(2 * p + 2) * tq] = pair_t[DH:].astype(BF16)
    qit_ref[...] = jnp.zeros(qit_ref.shape, BF16)
    heads_per_group = LANES // IDX_DIM
    for g in range(IDX_HEADS // heads_per_group):
        grp_t = iq_ref[0, :, g * LANES:(g + 1) * LANES].astype(F32).T
        for u in range(heads_per_group):
            h = g * heads_per_group + u
            qit_ref[0:IDX_DIM, h * tq:(h + 1) * tq] = grp_t[u * IDX_DIM:(u + 1) * IDX_DIM].astype(BF16)
    wt_ref[...] = iw_ref[0].T[0:SUBLANES]

    q_pos = t0 + lax.broadcasted_iota(I32, (tk, tq), 1)
    k_off = lax.broadcasted_iota(I32, (tk, tq), 0)

    def score_block(j, carry):
        k0 = pl.multiple_of(j * tk, tk)
        ikb = ik_ref[0, pl.ds(k0, tk), :]
        sc = jnp.zeros((tk, tq), F32)
        for c in range(IDX_HEADS // 2):
            lg = _dot(ikb, qit_ref[:, (2 * c) * tq:(2 * c + 2) * tq])
            for u in range(2):
                h = 2 * c + u
                sc = sc + jnp.maximum(lg[:, u * tq:(u + 1) * tq], 0.0) * wt_ref[h:h + 1, :]
        sc = sc + 0.0
        bits = pltpu.bitcast(sc, I32)
        key = bits ^ ((bits >> 31) & 0x7FFFFFFF)
        key = jnp.where(k0 + k_off <= q_pos, key, INT_MIN)
        key_ref[j] = key
        half_ref[j] = (key >> 16).astype(I16)
        return carry

    lax.fori_loop(0, nkb, score_block, 0)

    def search_half(rank):
        acc_rows = 4 * I16_ROWS

        def count_half_ge(cand):
            cand16 = cand.astype(I16)

            def body(j, cnt):
                hit = jnp.where(half_ref[j] >= cand16, jnp.int16(1), jnp.int16(0))
                for r in range(tk // acc_rows):
                    cnt = cnt + hit[r * acc_rows:(r + 1) * acc_rows]
                return cnt

            cnt = lax.fori_loop(0, nkb, body, jnp.zeros((acc_rows, tq), I16))
            return jnp.sum(cnt.astype(F32), axis=0, keepdims=True)

        def bit(it, thr_h):
            cand = thr_h + lax.shift_left(jnp.int32(1), 15 - it)
            return jnp.where(count_half_ge(cand) >= rank, cand, thr_h)

        thr_h = lax.fori_loop(0, 16, bit, jnp.full((1, tq), I16_MIN, I32))
        above = jnp.where(thr_h == I16_MAX, 0.0, count_half_ge(jnp.minimum(thr_h + 1, I16_MAX)))
        return thr_h, above

    thr_hi, n_above = search_half(top_k)
    base = thr_hi << 16

    def low_halves(j, carry):
        d = key_ref[j] - base
        low = jnp.where((d >> 16) == 0, d, 0) + I16_MIN
        half_ref[j] = low.astype(I16)
        return carry

    lax.fori_loop(0, nkb, low_halves, 0)
    thr_lo, _ = search_half(top_k - n_above)
    thr = base + (thr_lo - I16_MIN)

    def count_rows(hit_fn):
        acc_rows = 8 * SUBLANES

        def body(j, cnt):
            hit = hit_fn(j)
            return cnt + jnp.sum(hit.reshape(tk // acc_rows, acc_rows, tq), axis=0)

        cnt = lax.fori_loop(0, nkb, body, jnp.zeros((acc_rows, tq), F32))
        return jnp.sum(cnt, axis=0, keepdims=True)

    def count_ge(cand):
        return count_rows(lambda j: jnp.where(key_ref[j] >= cand, 1.0, 0.0))

    thr = jnp.maximum(thr, INT_MIN + 1)
    c_ge = count_ge(thr)
    c_gt = count_ge(thr + 1)
    excess = jnp.max(jnp.where(c_ge > top_k, 1.0, 0.0))

    @pl.when(excess > 0.0)
    def _():
        need = top_k - c_gt
        nbits = max(1, int(math.ceil(math.log2(key_ref.shape[0] * tk))))

        def count_eq_le(idx_lim):
            return count_rows(
                lambda j: jnp.where((key_ref[j] == thr) & (j * tk + k_off <= idx_lim), 1.0, 0.0))

        def idx_bit(it, lo):
            trial = lo + lax.shift_left(jnp.int32(1), nbits - 1 - it)
            return jnp.where(count_eq_le(trial - 1) < need, trial, lo)

        cut = lax.fori_loop(0, nbits, idx_bit, jnp.zeros((1, tq), I32))

        def drop(j, carry):
            key = key_ref[j]
            key_ref[j] = jnp.where((key == thr) & (j * tk + k_off > cut), INT_MIN, key)
            return carry

        lax.fori_loop(0, nkb, drop, 0)

    m_ref[...] = jnp.full(m_ref.shape, NEG_BIG, F32)
    l_ref[...] = jnp.zeros(l_ref.shape, F32)
    acc_ref[...] = jnp.zeros(acc_ref.shape, F32)
    cw = 2 * tq
    n_col = NH * tq
    rc = 4 * SUBLANES

    def attend(j, carry):
        k0 = pl.multiple_of(j * tk, tk)
        kvb = kv_ref[0, pl.ds(k0, tk), :]
        bias = jnp.where(key_ref[j] >= thr, 0.0, NEG_BIG)
        bias2 = jnp.concatenate([bias, bias], axis=1)
        maxes = []
        for c in range(NH // 2):
            cols = slice(c * cw, (c + 1) * cw)
            s = _dot(kvb, qt_ref[:, cols]) + bias2
            s_ref[:, cols] = s
            maxes.append(jnp.max(s, axis=0, keepdims=True))
        m_old = m_ref[...]
        m_new = jnp.maximum(m_old, jnp.concatenate(maxes, axis=1))
        alpha = jnp.exp2(m_old - m_new)
        m_ref[...] = m_new

        def exp_rows(r, psum):
            r0 = pl.multiple_of(r * rc, rc)
            p = jnp.exp2(s_ref[pl.ds(r0, rc), :] - m_new)
            p_ref[pl.ds(r0, rc), :] = p.astype(BF16)
            return psum + jnp.sum(p.reshape(rc // SUBLANES, SUBLANES, n_col), axis=0)

        psum = lax.fori_loop(0, tk // rc, exp_rows, jnp.zeros((SUBLANES, n_col), F32), unroll=2)
        l_ref[...] = l_ref[...] * alpha + jnp.sum(psum, axis=0, keepdims=True)
        vt = kvt_ref[0, j, DH:, :]
        for c in range(NH // 2):
            cols = slice(c * cw, (c + 1) * cw)
            acc_ref[:, cols] = acc_ref[:, cols] * alpha[:, cols] + _dot(vt, p_ref[:, cols])
        return carry

    lax.fori_loop(0, nkb, attend, 0)

    for p in range(NH // 2):
        pair_t = jnp.concatenate(
            [acc_ref[:, (2 * p + u) * tq:(2 * p + u + 1) * tq] / l_ref[:, (2 * p + u) * tq:(2 * p + u + 1) * tq]
             for u in range(2)], axis=0)
        o_ref[0, :, p * LANES:(p + 1) * LANES] = pair_t.T.astype(o_ref.dtype)


def _dsa(dq, dkv, dkvt, iq, ik, iw, tq, tk):
    B, L, _ = dq.shape
    top_k = min(DSA_TOPK_MAX, L // 4)
    nkb_max = L // tk
    assert dkvt.shape == (B, nkb_max, dkv.shape[2], tk)
    return pl.pallas_call(
        functools.partial(_dsa_kernel, tq=tq, tk=tk, top_k=float(top_k)),
        grid=(B, L // tq),
        in_specs=[pl.BlockSpec((1, tq, dq.shape[2]), lambda b, i: (b, i, 0)),
                  pl.BlockSpec((1, L, dkv.shape[2]), lambda b, i: (b, 0, 0)),
                  pl.BlockSpec((1,) + dkvt.shape[1:], lambda b, i: (b, 0, 0, 0)),
                  pl.BlockSpec((1, tq, iq.shape[2]), lambda b, i: (b, i, 0)),
                  pl.BlockSpec((1, L, ik.shape[2]), lambda b, i: (b, 0, 0)),
                  pl.BlockSpec((1, tq, iw.shape[2]), lambda b, i: (b, i, 0))],
        out_specs=pl.BlockSpec((1, tq, dq.shape[2]), lambda b, i: (b, i, 0)),
        out_shape=jax.ShapeDtypeStruct(dq.shape, BF16),
        scratch_shapes=[pltpu.VMEM((nkb_max, tk, tq), I32),
                        pltpu.VMEM((nkb_max, tk, tq), I16),
                        pltpu.VMEM((2 * DSA_HEAD_DIM, DSA_HEADS * tq), BF16),
                        pltpu.VMEM((LANES, IDX_HEADS * tq), BF16),
                        pltpu.VMEM((SUBLANES, tq), F32),
                        pltpu.VMEM((1, DSA_HEADS * tq), F32),
                        pltpu.VMEM((1, DSA_HEADS * tq), F32),
                        pltpu.VMEM((DSA_HEAD_DIM, DSA_HEADS * tq), F32),
                        pltpu.VMEM((tk, DSA_HEADS * tq), F32),
                        pltpu.VMEM((tk, DSA_HEADS * tq), BF16)],
        compiler_params=_cparams(("arbitrary", "arbitrary")),
        name="dsa",
    )(dq, dkv, dkvt, iq, ik, iw)


def _layer_norm(z, gain, bias):
    mu = jnp.mean(z, axis=-1, keepdims=True)
    zc = z - mu
    var = jnp.mean(zc * zc, axis=-1, keepdims=True)
    return zc * lax.rsqrt(var + LN_EPS) * gain + bias


def _merge_kernel(x_ref, yg_ref, yd_ref, wgate_ref, wbr_ref, wo_ref, gain_ref, bias_ref, o_ref, *, alpha):
    x = x_ref[0]
    D = x.shape[1]
    gates = jax.nn.sigmoid(_dot(x.astype(BF16), wgate_ref[...]))
    pg = _dot(yg_ref[0], wbr_ref[0])
    pd = _dot(yd_ref[0], wbr_ref[1])
    m = gates[:, 0:D] * pg + gates[:, D:] * pd
    mixed = _dot(m.astype(BF16), wo_ref[...])
    o_ref[0] = _layer_norm(alpha * x + mixed, gain_ref[...], bias_ref[...])


def _merge(x, yg, yd, wgate, wbr, wo, gain, bias, rows, alpha):
    B, L, D = x.shape

    def full(a):
        return pl.BlockSpec(a.shape, lambda b, i, _n=a.ndim: (0,) * _n)

    def rowblk(w):
        return pl.BlockSpec((1, rows, w), lambda b, i: (b, i, 0))

    return pl.pallas_call(
        functools.partial(_merge_kernel, alpha=alpha),
        grid=(B, L // rows),
        in_specs=[rowblk(D), rowblk(yg.shape[2]), rowblk(yd.shape[2]),
                  full(wgate), full(wbr), full(wo), full(gain), full(bias)],
        out_specs=rowblk(D),
        out_shape=jax.ShapeDtypeStruct((B, L, D), F32),
        compiler_params=_cparams(("arbitrary", "arbitrary")),
        name="merge_ln1",
    )(x, yg, yd, wgate, wbr, wo, gain, bias)


def _ffn_kernel(x_ref, wup_ref, cw_ref, cb_ref, wdn_ref, gain_ref, bias_ref, o_ref, hist_ref, u_ref, act_ref,
                *, alpha, n_chunks, cw, rsub):
    rows = x_ref.shape[1]
    H = SUBLANES

    @pl.when(pl.program_id(1) == 0)
    def _():
        hist_ref[...] = jnp.zeros_like(hist_ref)

    xb = x_ref[0].astype(BF16)

    def up_proj(c, slot):
        u_ref[slot, 0:H, :] = hist_ref[c]
        u_ref[slot, H:, :] = _dot(xb, wup_ref[c])
        hist_ref[c] = u_ref[slot, rows:rows + H, :]

    def conv_gate(c, slot):
        w = cw_ref[c]
        b = cb_ref[c]
        for r in range(rows // rsub):
            base = H + r * rsub
            cu = (b + u_ref[slot, base - 2:base - 2 + rsub, :] * w[0:1, :]
                  + u_ref[slot, base - 1:base - 1 + rsub, :] * w[1:2, :]
                  + u_ref[slot, base:base + rsub, :] * w[2:3, :])
            ug = cu[:, 0:cw]
            act = (ug * jax.nn.sigmoid(ug)) * cu[:, cw:]
            act_ref[c, r * rsub:(r + 1) * rsub, :] = act.astype(BF16)

    up_proj(0, 0)
    for c in range(1, n_chunks):
        up_proj(c, c % 2)
        conv_gate(c - 1, (c - 1) % 2)
    conv_gate(n_chunks - 1, (n_chunks - 1) % 2)
    act = jnp.concatenate([act_ref[c] for c in range(n_chunks)], axis=1)
    f = _dot(act, wdn_ref[...])
    o_ref[0] = _layer_norm(alpha * x_ref[0] + f, gain_ref[...], bias_ref[...])


def _ffn(x, wup, cwt, cbs, wdn, gain, bias, rows, alpha):
    B, L, D = x.shape
    n_chunks, _, cw2 = wup.shape

    def full(a):
        return pl.BlockSpec(a.shape, lambda b, i, _n=a.ndim: (0,) * _n, pipeline_mode=pl.Buffered(1))

    rowblk = pl.BlockSpec((1, rows, D), lambda b, i: (b, i, 0))
    return pl.pallas_call(
        functools.partial(_ffn_kernel, alpha=alpha, n_chunks=n_chunks, cw=cw2 // 2, rsub=min(rows, 64)),
        grid=(B, L // rows),
        in_specs=[rowblk, full(wup), full(cwt), full(cbs), full(wdn), full(gain), full(bias)],
        out_specs=rowblk,
        out_shape=jax.ShapeDtypeStruct((B, L, D), F32),
        scratch_shapes=[pltpu.VMEM((n_chunks, SUBLANES, cw2), F32),
                        pltpu.VMEM((2, rows + SUBLANES, cw2), F32),
                        pltpu.VMEM((n_chunks, rows, cw2 // 2), BF16)],
        compiler_params=_cparams(("arbitrary", "arbitrary")),
        name="ffn_ln2",
    )(x, wup, cwt, cbs, wdn, gain, bias)


def _pad_cols(w, width):
    return jnp.pad(w, ((0, 0), (0, width - w.shape[1])))


def _ffn_chunk_width(d_ff):
    for cw in (512, 384, 256, 128):
        if d_ff % cw == 0:
            return cw
    raise ValueError(f"d_ff={d_ff} must be a multiple of {LANES}")


def _token_mixers(x, w_in, w_a2, b_a, gla_gain):
    B, L, D = x.shape
    nqk = GLA_HEADS * GLA_DK
    nv = GLA_HEADS * GLA_DV
    ndq = DSA_HEADS * DSA_HEAD_DIM
    niq = IDX_HEADS * IDX_DIM
    widths = (nqk, nqk, nv, GLA_GATE_RANK, nv, ndq, DSA_HEAD_DIM, DSA_HEAD_DIM, niq, IDX_DIM, IDX_HEADS, N_BRANCH * D)
    offs = np.concatenate([[0], np.cumsum(widths)])
    assert w_in.shape == (D, int(offs[-1]))
    col = lambda n: w_in[:, int(offs[n]):int(offs[n + 1])]
    w_gq, w_gk, w_gv, w_ga, w_gr, w_dq, w_dk, w_dv, w_iq, w_ik, w_iw, w_gate = [col(n) for n in range(12)]
    wts = (
        jnp.concatenate([w_gq, w_gk, w_gv, w_gr], axis=1).astype(BF16),
        _pad_cols(w_ga, LANES).astype(BF16),
        jnp.pad(w_a2, ((0, LANES - GLA_GATE_RANK), (0, 0))).astype(BF16),
        b_a.reshape(1, nqk).astype(F32),
        w_dq.astype(BF16),
        jnp.concatenate([w_dk, w_dv], axis=1).astype(BF16),
        w_iq.astype(BF16),
        _pad_cols(w_ik, LANES).astype(BF16),
        _pad_cols(w_iw, LANES).astype(BF16),
    )
    def lane_freqs(head_dim):
        rot = head_dim // ROPE_FRACTION
        half = rot // 2
        inv = ROPE_THETA ** (-jnp.arange(half, dtype=F32) * 2.0 / rot)
        m = np.arange(LANES) % head_dim
        return jnp.where(jnp.asarray(m < rot), inv[jnp.asarray(m % half)], 0.0)

    invf = jnp.stack([lane_freqs(DSA_HEAD_DIM), lane_freqs(IDX_DIM)]).reshape(2, 1, LANES)
    tabs = _rope_tables(L, invf)

    tq = LANES
    tk = min(L, 512)
    gla_in, g, dq, dkv, dkvt, iq, ik, iw = _in_proj(x, wts, tabs, rows=tk)
    y_gla = _gla(gla_in, g, gla_gain.reshape(1, nv).astype(F32), rows=min(L, 512))
    y_dsa = _dsa(dq, dkv, dkvt, iq, ik, iw, tq, tk)
    return y_gla, y_dsa, w_gate


def _layer(x, w_in, w_a2, b_a, gla_gain, w_branch, w_o, ln1_g, ln1_b, w_up, conv_w, conv_b, w_down, ln2_g, ln2_b,
           depth):
    B, L, D = x.shape
    alpha = (2.0 * depth) ** 0.25
    y_gla, y_dsa, w_gate = _token_mixers(x, w_in, w_a2, b_a, gla_gain)
    x1 = _merge(x, y_gla, y_dsa, w_gate.astype(BF16), w_branch.astype(BF16), w_o.astype(BF16),
                ln1_g.reshape(1, D), ln1_b.reshape(1, D), rows=min(L, 512), alpha=alpha)

    d_ff = w_down.shape[0]
    cw = _ffn_chunk_width(d_ff)
    nch = d_ff // cw
    chunked = lambda a: jnp.concatenate(
        [a[..., :d_ff].reshape(a.shape[:-1] + (nch, cw)), a[..., d_ff:].reshape(a.shape[:-1] + (nch, cw))], axis=-1)
    wup = jnp.moveaxis(chunked(w_up), 1, 0).astype(BF16)
    cwt = jnp.moveaxis(chunked(conv_w), 1, 0)
    cwt = jnp.pad(cwt, ((0, 0), (0, SUBLANES - CONV_WIDTH), (0, 0)))
    cbs = jnp.moveaxis(chunked(conv_b[None, :]), 1, 0)
    return _ffn(x1, wup, cwt, cbs, w_down.astype(BF16), ln2_g.reshape(1, D), ln2_b.reshape(1, D),
                rows=min(L, 512), alpha=alpha)


def kernel(x, w_in, w_gla_a2, b_gla_a, gla_norm_gain, w_branch, w_o, ln1_gain, ln1_bias,
           w_up, conv_w, conv_b, w_down, ln2_gain, ln2_bias):
    depth = w_in.shape[0]
    for layer in range(depth):
        x = _layer(x, w_in[layer], w_gla_a2[layer], b_gla_a[layer], gla_norm_gain[layer], w_branch[layer],
                   w_o[layer], ln1_gain[layer], ln1_bias[layer], w_up[layer], conv_w[layer], conv_b[layer],
                   w_down[layer], ln2_gain[layer], ln2_bias[layer], depth)
    return x
```

```python
import functools
import math

import numpy as np
import jax
import jax.numpy as jnp
from jax import lax
from jax.experimental import pallas as pl
from jax.experimental.pallas import tpu as pltpu

F32 = jnp.float32
BF16 = jnp.bfloat16
I32 = jnp.int32

GLA_HEADS = 4
GLA_DK = 64
GLA_DV = 128
GLA_GATE_RANK = 16
GLA_TAU = 16.0
DSA_HEADS = 8
DSA_HEAD_DIM = 64
IDX_HEADS = 8
IDX_DIM = 32
DSA_TOPK_MAX = 256
ROPE_THETA = 500000.0
ROPE_FRACTION = 4
N_BRANCH = 2
BRANCH_WIDTH = 512
CONV_WIDTH = 3
LN_EPS = 1e-5
RMS_EPS = 1e-6

LANES = 128
SUBLANES = 8
VMEM_LIMIT_BYTES = 56 * 1024 * 1024

GLA_CHUNK = 64
GLA_SUB = 16
WORD_BITS = 32
INT_MIN = -(2 ** 31)
NEG_BIG = -1e30
LOG2_E = 1.4426950408889634


def _cparams(sem):
    return pltpu.CompilerParams(dimension_semantics=sem, vmem_limit_bytes=VMEM_LIMIT_BYTES)


def _log2(n):
    assert n > 0 and n & (n - 1) == 0, f"{n} must be a power of two"
    return n.bit_length() - 1


def _div_pow2(x, n):
    return lax.shift_right_logical(x, jnp.int32(_log2(n)))


def _mod_pow2(x, n):
    _log2(n)
    return x & (n - 1)


def _dot(a, b):
    return jnp.dot(a, b, preferred_element_type=F32)


def _dot_nt(a, b):
    return lax.dot_general(a, b, (((1,), (1,)), ((), ())), preferred_element_type=F32)


def _dot_tn(a, b):
    return lax.dot_general(a, b, (((0,), (0,)), ((), ())), preferred_element_type=F32)


def _rope_table_kernel(invf_ref, c_ref, sa_ref, sb_ref, *, rows, periods, halves):
    t0 = pl.program_id(0) * rows
    pos = (t0 + lax.broadcasted_iota(I32, (rows, LANES), 0)).astype(F32)
    lane = lax.broadcasted_iota(I32, (rows, LANES), 1)
    for p, (period, half) in enumerate(zip(periods, halves)):
        ang = pos * invf_ref[p]
        c = jnp.cos(ang)
        s = jnp.sin(ang)
        m = _mod_pow2(lane, period)
        is1 = m < half
        is2 = (m >= half) & (m < 2 * half)
        c_ref[p] = jnp.where(is1 | is2, c, 1.0)
        sa_ref[p] = jnp.where(is1, -s, 0.0)
        sb_ref[p] = jnp.where(is2, s, 0.0)


def _rope_tables(L, invf):
    rows = min(L, 512)
    n = invf.shape[0]
    spec = pl.BlockSpec((n, rows, LANES), lambda i: (0, i, 0))
    shp = jax.ShapeDtypeStruct((n, L, LANES), F32)
    return pl.pallas_call(
        functools.partial(_rope_table_kernel, rows=rows, periods=(DSA_HEAD_DIM, IDX_DIM),
                          halves=(DSA_HEAD_DIM // ROPE_FRACTION // 2, IDX_DIM // ROPE_FRACTION // 2)),
        grid=(L // rows,),
        in_specs=[pl.BlockSpec((n, 1, LANES), lambda i: (0, 0, 0))],
        out_specs=[spec, spec, spec],
        out_shape=[shp, shp, shp],
        compiler_params=_cparams(("arbitrary",)),
        name="rope_tables",
    )(invf)


def _apply_rope(x, c, sa, sb, half, lane_limit=None):
    outs = []
    for g in range(x.shape[1] // LANES):
        xs = x[:, g * LANES:(g + 1) * LANES]
        up = pltpu.roll(xs, LANES - half, axis=1)
        dn = pltpu.roll(xs, half, axis=1)
        y = xs * c + up * sa + dn * sb
        if lane_limit is not None:
            lane = lax.broadcasted_iota(I32, xs.shape, 1)
            y = jnp.where(lane < lane_limit, y, xs)
        outs.append(y)
    return outs[0] if len(outs) == 1 else jnp.concatenate(outs, axis=1)


def _in_proj_kernel(x_ref, wgla_ref, wga_ref, wa2_ref, ba_ref, wdq_ref, wdkv_ref, wiq_ref, wik_ref, wiw_ref,
                    c_ref, sa_ref, sb_ref,
                    gla_ref, g_ref, dq_ref, dkv_ref, dkvt_ref, iq_ref, ik_ref, iw_ref):
    xb = x_ref[0].astype(BF16)
    hg = _dot(xb, wgla_ref[...])
    nqk = GLA_HEADS * GLA_DK
    gla_ref[0, :, 0:nqk] = (hg[:, 0:nqk] * (GLA_DK ** -0.5)).astype(BF16)
    gla_ref[0, :, nqk:] = hg[:, nqk:].astype(BF16)
    a_low = _dot(xb, wga_ref[...])
    z = _dot(a_low.astype(BF16), wa2_ref[...]) + ba_ref[...]
    g_ref[0] = (jnp.minimum(z, 0.0) - jnp.log1p(jnp.exp(-jnp.abs(z)))) * (1.0 / GLA_TAU)
    c64, sa64, sb64 = c_ref[0], sa_ref[0], sb_ref[0]
    c32, sa32, sb32 = c_ref[1], sa_ref[1], sb_ref[1]
    h64 = DSA_HEAD_DIM // ROPE_FRACTION // 2
    h32 = IDX_DIM // ROPE_FRACTION // 2
    dq = _dot(xb, wdq_ref[...])
    dq_ref[0] = (_apply_rope(dq, c64, sa64, sb64, h64) * (DSA_HEAD_DIM ** -0.5 * LOG2_E)).astype(BF16)
    dkv = _apply_rope(_dot(xb, wdkv_ref[...]), c64, sa64, sb64, h64, lane_limit=DSA_HEAD_DIM)
    dkv_ref[0] = dkv.astype(BF16)
    dkvt_ref[0, 0] = dkv.T.astype(BF16)
    iq = _dot(xb, wiq_ref[...])
    iq_ref[0] = _apply_rope(iq, c32, sa32, sb32, h32).astype(BF16)
    ik = _dot(xb, wik_ref[...])
    ik_ref[0] = _apply_rope(ik, c32, sa32, sb32, h32).astype(BF16)
    iw_ref[0] = _dot(xb, wiw_ref[...]) * ((IDX_HEADS ** -0.5) * (IDX_DIM ** -0.5))


def _in_proj(x, wts, tabs, rows):
    B, L, D = x.shape
    wgla, wga, wa2, ba, wdq, wdkv, wiq, wik, wiw = wts
    c, sa, sb = tabs
    n_l = L // rows

    def full(a):
        return pl.BlockSpec(a.shape, lambda i, b, _n=a.ndim: (0,) * _n)

    def rowblk(w):
        return pl.BlockSpec((1, rows, w), lambda i, b: (b, i, 0))

    tab = pl.BlockSpec((2, rows, LANES), lambda i, b: (0, i, 0))
    widths = (wgla.shape[1], GLA_HEADS * GLA_DK, wdq.shape[1], wdkv.shape[1], wiq.shape[1], wik.shape[1], wiw.shape[1])
    dtypes = (BF16, F32, BF16, BF16, BF16, BF16, F32)
    out_specs = [rowblk(w) for w in widths]
    out_shape = [jax.ShapeDtypeStruct((B, L, w), dt) for w, dt in zip(widths, dtypes)]
    out_specs.insert(4, pl.BlockSpec((1, 1, wdkv.shape[1], rows), lambda i, b: (b, i, 0, 0)))
    out_shape.insert(4, jax.ShapeDtypeStruct((B, n_l, wdkv.shape[1], rows), BF16))
    return pl.pallas_call(
        _in_proj_kernel,
        grid=(n_l, B),
        in_specs=[rowblk(D)] + [full(w) for w in wts] + [tab, tab, tab],
        out_specs=out_specs,
        out_shape=out_shape,
        compiler_params=_cparams(("arbitrary", "arbitrary")),
        name="in_proj",
    )(x, *wts, c, sa, sb)


def _gla_kernel(qkvr_ref, g_ref, gain_ref, tril_ref, expand_ref, y_ref, st_ref, *, rows):
    C, S = GLA_CHUNK, GLA_SUB
    H, DK, DV = GLA_HEADS, GLA_DK, GLA_DV
    HK, HV = H * DK, H * DV

    @pl.when(pl.program_id(1) == 0)
    def _():
        st_ref[...] = jnp.zeros_like(st_ref)

    ri = lax.broadcasted_iota(I32, (C, HK), 0)
    li = lax.broadcasted_iota(I32, (C, HK), 1)
    head_of_lane = _div_pow2(li, DK)
    sr = _div_pow2(lax.broadcasted_iota(I32, (HV, HK), 0), DV)
    scol = _div_pow2(lax.broadcasted_iota(I32, (HV, HK), 1), DK)
    st_mask = sr == scol
    tril = tril_ref[...]
    expand = expand_ref[...]
    gain = gain_ref[...]

    def chunk(ci, carry):
        r0 = pl.multiple_of(ci * C, C)
        blk = qkvr_ref[0, pl.ds(r0, C), :]
        q = blk[:, 0:HK].astype(F32)
        k = blk[:, HK:2 * HK].astype(F32)
        v = blk[:, 2 * HK:2 * HK + HV].astype(F32)
        r = blk[:, 2 * HK + HV:].astype(F32)
        g = g_ref[0, pl.ds(r0, C), :]
        g1 = g.astype(BF16)
        rem = g - g1.astype(F32)
        g2 = rem.astype(BF16)
        g3 = (rem - g2.astype(F32)).astype(BF16)
        b = _dot(tril, g1) + _dot(tril, g2) + _dot(tril, g3)
        b_last = b[C - 1:C, :]
        st = st_ref[...]
        o = _dot_nt((q * jnp.exp(b)).astype(BF16), st.astype(BF16))
        p_acc = jnp.zeros((H * C, C), F32)
        for I in range(1, C // S):
            beta = b[I * S - 1:I * S, :]
            in_I = (ri >= I * S) & (ri < (I + 1) * S)
            qI = jnp.where(in_I, q * jnp.exp(jnp.minimum(b - beta, 0.0)), 0.0)
            kI = jnp.where(ri < I * S, k * jnp.exp(jnp.minimum(beta - b, 0.0)), 0.0)
            q_st = jnp.concatenate([jnp.where(head_of_lane == h, qI, 0.0) for h in range(H)], axis=0)
            p_acc = p_acc + _dot_nt(q_st.astype(BF16), kI.astype(BF16))
        p_b = p_acc.astype(BF16)
        vb = v.astype(BF16)
        o_off = jnp.concatenate(
            [_dot(p_b[h * C:(h + 1) * C, :], vb[:, h * DV:(h + 1) * DV]) for h in range(H)], axis=1)
        o = o + o_off
        GRP = 4
        for d0 in range(0, S, GRP):
            ts = []
            for d in range(d0, d0 + GRP):
                if d == 0:
                    ts.append(q * k)
                else:
                    kd = pltpu.roll(k, d, axis=0)
                    bd = pltpu.roll(b, d, axis=0)
                    valid = _mod_pow2(ri, S) >= d
                    ts.append(q * kd * jnp.exp(jnp.where(valid, jnp.minimum(b - bd, 0.0), -jnp.inf)))
            s_all = _dot(jnp.concatenate(ts, axis=0).astype(BF16), expand)
            for n, d in enumerate(range(d0, d0 + GRP)):
                vd = v if d == 0 else pltpu.roll(v, d, axis=0)
                o = o + s_all[n * C:(n + 1) * C, :] * vd
        k_end = k * jnp.exp(b_last - b)
        upd = _dot_tn(vb, k_end.astype(BF16))
        st_ref[...] = st * jnp.exp(b_last) + jnp.where(st_mask, upd, 0.0)
        outs = []
        for h in range(H):
            oh = o[:, h * DV:(h + 1) * DV]
            ms = jnp.mean(oh * oh, axis=-1, keepdims=True)
            outs.append(oh * lax.rsqrt(ms + RMS_EPS))
        on = jnp.concatenate(outs, axis=1) * gain
        y = on * (r * jax.nn.sigmoid(r))
        y_ref[0, pl.ds(r0, C), :] = y.astype(y_ref.dtype)
        return carry

    lax.fori_loop(0, rows // C, chunk, 0)


def _gla(qkvr, g, gain, rows):
    B, L, W = qkvr.shape
    HK, HV = GLA_HEADS * GLA_DK, GLA_HEADS * GLA_DV
    C = GLA_CHUNK
    tril = jnp.asarray(np.tril(np.ones((C, C), np.float32)), BF16)
    expand = jnp.asarray(np.kron(np.eye(GLA_HEADS, dtype=np.float32), np.ones((GLA_DK, GLA_DV), np.float32)), BF16)
    return pl.pallas_call(
        functools.partial(_gla_kernel, rows=rows),
        grid=(B, L // rows),
        in_specs=[pl.BlockSpec((1, rows, W), lambda b, i: (b, i, 0)),
                  pl.BlockSpec((1, rows, HK), lambda b, i: (b, i, 0)),
                  pl.BlockSpec((1, HV), lambda b, i: (0, 0)),
                  pl.BlockSpec((C, C), lambda b, i: (0, 0)),
                  pl.BlockSpec((HK, HV), lambda b, i: (0, 0))],
        out_specs=pl.BlockSpec((1, rows, HV), lambda b, i: (b, i, 0)),
        out_shape=jax.ShapeDtypeStruct((B, L, HV), BF16),
        scratch_shapes=[pltpu.VMEM((HV, HK), F32)],
        compiler_params=_cparams(("arbitrary", "arbitrary")),
        name="gla",
    )(qkvr, g, gain, tril, expand)


def _bit_transpose32(words):
    a = list(words)
    assert len(a) == WORD_BITS
    j, m = WORD_BITS // 2, 0x0000FFFF
    while j:
        m_i32 = m - (1 << 32) if m >= (1 << 31) else m
        for k in range(WORD_BITS):
            if k & j == 0:
                t = (a[k] ^ lax.shift_right_logical(a[k + j], jnp.int32(j))) & m_i32
                a[k] = a[k] ^ t
                a[k + j] = a[k + j] ^ lax.shift_left(t, jnp.int32(j))
        j >>= 1
        m = (m ^ (m << j)) & 0xFFFFFFFF
    return a


def _dsa_kernel(dq_ref, kv_ref, kvt_ref, iq_ref, ik_ref, iw_ref, o_ref,
                key_ref, plane_ref, cand_ref, qt_ref, qit_ref, wt_ref, m_ref, l_ref, acc_ref, s_ref, p_ref,
                *, tq, tk, top_k):
    NH, DH = DSA_HEADS, DSA_HEAD_DIM
    assert tq == LANES
    i = pl.program_id(1)
    t0 = i * tq
    nkb = (t0 + tq + tk - 1) // tk

    qt_ref[DH:, :] = jnp.zeros((DH, NH * tq), BF16)
    for p in range(NH // 2):
        pair_t = dq_ref[0, :, p * LANES:(p + 1) * LANES].astype(F32).T
        qt_ref[0:DH, (2 * p) * tq:(2 * p + 1) * tq] = pair_t[0:DH].astype(BF16)
        qt_ref[0:DH, (2 * p + 1) * tq:(2 * p + 2) * tq] = pair_t[DH:].astype(BF16)
    qit_ref[...] = jnp.zeros(qit_ref.shape, BF16)
    heads_per_group = LANES // IDX_DIM
    for g in range(IDX_HEADS // heads_per_group):
        grp_t = iq_ref[0, :, g * LANES:(g + 1) * LANES].astype(F32).T
        for u in range(heads_per_group):
            h = g * heads_per_group + u
            qit_ref[0:IDX_DIM, h * tq:(h + 1) * tq] = grp_t[u * IDX_DIM:(u + 1) * IDX_DIM].astype(BF16)
    wt_ref[...] = iw_ref[0].T[0:SUBLANES]

    q_pos = t0 + lax.broadcasted_iota(I32, (tk, tq), 1)
    k_off = lax.broadcasted_iota(I32, (tk, tq), 0)
    q_pos8 = t0 + lax.broadcasted_iota(I32, (SUBLANES, tq), 1)
    sub8 = lax.broadcasted_iota(I32, (SUBLANES, tq), 0)
    groups_per_block = tk // (WORD_BITS * SUBLANES)

    @pl.when((pl.program_id(0) == 0) & (i == 0))
    def _():
        plane_ref[...] = jnp.zeros(plane_ref.shape, I32)

    def score_block(j, carry):
        k0 = pl.multiple_of(j * tk, tk)
        ikb = ik_ref[0, pl.ds(k0, tk), :]
        sc = jnp.zeros((tk, tq), F32)
        for c in range(IDX_HEADS // 2):
            lg = _dot(ikb, qit_ref[:, (2 * c) * tq:(2 * c + 2) * tq])
            for u in range(2):
                h = 2 * c + u
                sc = sc + jnp.maximum(lg[:, u * tq:(u + 1) * tq], 0.0) * wt_ref[h:h + 1, :]
        sc = sc + 0.0
        bits = pltpu.bitcast(sc, I32)
        key = bits ^ ((bits >> 31) & 0x7FFFFFFF)
        key = jnp.where(k0 + k_off <= q_pos, key, INT_MIN)
        key_ref[j] = key
        ukey = key ^ INT_MIN
        for g in range(groups_per_block):
            words = [ukey[(g * WORD_BITS + w) * SUBLANES:(g * WORD_BITS + w + 1) * SUBLANES, :]
                     for w in range(WORD_BITS)]
            planes = _bit_transpose32(words)
            r0 = pl.multiple_of((j * groups_per_block + g) * SUBLANES, SUBLANES)
            for p in range(WORD_BITS):
                plane_ref[p, pl.ds(r0, SUBLANES), :] = planes[p]
            n_adm = ((q_pos8 - (k0 + g * WORD_BITS * SUBLANES) - sub8) >> _log2(SUBLANES)) + 1
            n_adm = jnp.clip(n_adm, 0, WORD_BITS)
            cand_ref[pl.ds(r0, SUBLANES), :] = jnp.where(
                n_adm == 0, 0, lax.shift_left(jnp.int32(-1), WORD_BITS - jnp.maximum(n_adm, 1)))
        return carry

    cand_ref[...] = jnp.zeros(cand_ref.shape, I32)
    lax.fori_loop(0, nkb, score_block, 0)

    def popcount_rows(words):
        pc = lax.population_count(words)
        parts = [pc[r * SUBLANES:(r + 1) * SUBLANES, :] for r in range(words.shape[0] // SUBLANES)]
        while len(parts) > 1:
            parts = [a + b for a, b in zip(parts[0::2], parts[1::2])]
        return jnp.sum(parts[0].astype(F32), axis=0, keepdims=True)

    def select_bit(p, carry):
        thr_u, rank = carry
        plane = plane_ref[p]
        cand = cand_ref[...]
        ones = cand & plane
        n_ones = popcount_rows(ones)
        take = n_ones >= rank
        thr_u = thr_u | jnp.where(take, lax.shift_left(jnp.int32(1), WORD_BITS - 1 - p), 0)
        rank = jnp.where(take, rank, rank - n_ones)
        cand_ref[...] = cand & (plane ^ jnp.where(take, 0, -1))
        return thr_u, rank

    thr_u, rank = lax.fori_loop(0, WORD_BITS, select_bit,
                                (jnp.zeros((1, tq), I32), jnp.full((1, tq), top_k, F32)))
    thr = thr_u ^ INT_MIN
    n_tied = popcount_rows(cand_ref[...])

    def count_rows(hit_fn):
        acc_rows = 8 * SUBLANES

        def body(j, cnt):
            hit = hit_fn(j)
            return cnt + jnp.sum(hit.reshape(tk // acc_rows, acc_rows, tq), axis=0)

        cnt = lax.fori_loop(0, nkb, body, jnp.zeros((acc_rows, tq), F32))
        return jnp.sum(cnt, axis=0, keepdims=True)

    thr = jnp.maximum(thr, INT_MIN + 1)
    excess = jnp.max(jnp.where(n_tied > rank, 1.0, 0.0))

    @pl.when(excess > 0.0)
    def _():
        need = rank
        nbits = max(1, int(math.ceil(math.log2(key_ref.shape[0] * tk))))

        def count_eq_le(idx_lim):
            return count_rows(
                lambda j: jnp.where((key_ref[j] == thr) & (j * tk + k_off <= idx_lim), 1.0, 0.0))

        def idx_bit(it, lo):
            trial = lo + lax.shift_left(jnp.int32(1), nbits - 1 - it)
            return jnp.where(count_eq_le(trial - 1) < need, trial, lo)

        cut = lax.fori_loop(0, nbits, idx_bit, jnp.zeros((1, tq), I32))

        def drop(j, carry):
            key = key_ref[j]
            key_ref[j] = jnp.where((key == thr) & (j * tk + k_off > cut), INT_MIN, key)
            return carry

        lax.fori_loop(0, nkb, drop, 0)

    m_ref[...] = jnp.full(m_ref.shape, NEG_BIG, F32)
    l_ref[...] = jnp.zeros(l_ref.shape, F32)
    acc_ref[...] = jnp.zeros(acc_ref.shape, F32)
    cw = 2 * tq
    n_col = NH * tq
    rc = 4 * SUBLANES

    def attend(j, carry):
        k0 = pl.multiple_of(j * tk, tk)
        kvb = kv_ref[0, pl.ds(k0, tk), :]
        bias = jnp.where(key_ref[j] >= thr, 0.0, NEG_BIG)
        bias2 = jnp.concatenate([bias, bias], axis=1)
        maxes = []
        for c in range(NH // 2):
            cols = slice(c * cw, (c + 1) * cw)
            s = _dot(kvb, qt_ref[:, cols]) + bias2
            s_ref[:, cols] = s
            maxes.append(jnp.max(s, axis=0, keepdims=True))
        m_old = m_ref[...]
        m_new = jnp.maximum(m_old, jnp.concatenate(maxes, axis=1))
        alpha = jnp.exp2(m_old - m_new)
        m_ref[...] = m_new

        def exp_rows(r, psum):
            r0 = pl.multiple_of(r * rc, rc)
            p = jnp.exp2(s_ref[pl.ds(r0, rc), :] - m_new)
            p_ref[pl.ds(r0, rc), :] = p.astype(BF16)
            return psum + jnp.sum(p.reshape(rc // SUBLANES, SUBLANES, n_col), axis=0)

        psum = lax.fori_loop(0, tk // rc, exp_rows, jnp.zeros((SUBLANES, n_col), F32), unroll=2)
        l_ref[...] = l_ref[...] * alpha + jnp.sum(psum, axis=0, keepdims=True)
        vt = kvt_ref[0, j, DH:, :]
        for c in range(NH // 2):
            cols = slice(c * cw, (c + 1) * cw)
            acc_ref[:, cols] = acc_ref[:, cols] * alpha[:, cols] + _dot(vt, p_ref[:, cols])
        return carry

    lax.fori_loop(0, nkb, attend, 0)

    for p in range(NH // 2):
        pair_t = jnp.concatenate(
            [acc_ref[:, (2 * p + u) * tq:(2 * p + u + 1) * tq] / l_ref[:, (2 * p + u) * tq:(2 * p + u + 1) * tq]
             for u in range(2)], axis=0)
        o_ref[0, :, p * LANES:(p + 1) * LANES] = pair_t.T.astype(o_ref.dtype)


def _dsa(dq, dkv, dkvt, iq, ik, iw, tq, tk):
    B, L, _ = dq.shape
    top_k = min(DSA_TOPK_MAX, L // 4)
    nkb_max = L // tk
    assert dkvt.shape == (B, nkb_max, dkv.shape[2], tk)
    return pl.pallas_call(
        functools.partial(_dsa_kernel, tq=tq, tk=tk, top_k=float(top_k)),
        grid=(B, L // tq),
        in_specs=[pl.BlockSpec((1, tq, dq.shape[2]), lambda b, i: (b, i, 0)),
                  pl.BlockSpec((1, L, dkv.shape[2]), lambda b, i: (b, 0, 0)),
                  pl.BlockSpec((1,) + dkvt.shape[1:], lambda b, i: (b, 0, 0, 0)),
                  pl.BlockSpec((1, tq, iq.shape[2]), lambda b, i: (b, i, 0)),
                  pl.BlockSpec((1, L, ik.shape[2]), lambda b, i: (b, 0, 0)),
                  pl.BlockSpec((1, tq, iw.shape[2]), lambda b, i: (b, i, 0))],
        out_specs=pl.BlockSpec((1, tq, dq.shape[2]), lambda b, i: (b, i, 0)),
        out_shape=jax.ShapeDtypeStruct(dq.shape, BF16),
        scratch_shapes=[pltpu.VMEM((nkb_max, tk, tq), I32),
                        pltpu.VMEM((WORD_BITS, L // WORD_BITS, tq), I32),
                        pltpu.VMEM((L // WORD_BITS, tq), I32),
                        pltpu.VMEM((2 * DSA_HEAD_DIM, DSA_HEADS * tq), BF16),
                        pltpu.VMEM((LANES, IDX_HEADS * tq), BF16),
                        pltpu.VMEM((SUBLANES, tq), F32),
                        pltpu.VMEM((1, DSA_HEADS * tq), F32),
                        pltpu.VMEM((1, DSA_HEADS * tq), F32),
                        pltpu.VMEM((DSA_HEAD_DIM, DSA_HEADS * tq), F32),
                        pltpu.VMEM((tk, DSA_HEADS * tq), F32),
                        pltpu.VMEM((tk, DSA_HEADS * tq), BF16)],
        compiler_params=_cparams(("arbitrary", "arbitrary")),
        name="dsa",
    )(dq, dkv, dkvt, iq, ik, iw)


def _layer_norm(z, gain, bias):
    mu = jnp.mean(z, axis=-1, keepdims=True)
    zc = z - mu
    var = jnp.mean(zc * zc, axis=-1, keepdims=True)
    return zc * lax.rsqrt(var + LN_EPS) * gain + bias


def _merge_kernel(x_ref, yg_ref, yd_ref, wgate_ref, wbr_ref, wo_ref, gain_ref, bias_ref, o_ref, *, alpha):
    x = x_ref[0]
    D = x.shape[1]
    gates = jax.nn.sigmoid(_dot(x.astype(BF16), wgate_ref[...]))
    pg = _dot(yg_ref[0], wbr_ref[0])
    pd = _dot(yd_ref[0], wbr_ref[1])
    m = gates[:, 0:D] * pg + gates[:, D:] * pd
    mixed = _dot(m.astype(BF16), wo_ref[...])
    o_ref[0] = _layer_norm(alpha * x + mixed, gain_ref[...], bias_ref[...])


def _merge(x, yg, yd, wgate, wbr, wo, gain, bias, rows, alpha):
    B, L, D = x.shape

    def full(a):
        return pl.BlockSpec(a.shape, lambda b, i, _n=a.ndim: (0,) * _n)

    def rowblk(w):
        return pl.BlockSpec((1, rows, w), lambda b, i: (b, i, 0))

    return pl.pallas_call(
        functools.partial(_merge_kernel, alpha=alpha),
        grid=(B, L // rows),
        in_specs=[rowblk(D), rowblk(yg.shape[2]), rowblk(yd.shape[2]),
                  full(wgate), full(wbr), full(wo), full(gain), full(bias)],
        out_specs=rowblk(D),
        out_shape=jax.ShapeDtypeStruct((B, L, D), F32),
        compiler_params=_cparams(("arbitrary", "arbitrary")),
        name="merge_ln1",
    )(x, yg, yd, wgate, wbr, wo, gain, bias)


def _ffn_kernel(x_ref, wup_ref, cw_ref, cb_ref, wdn_ref, gain_ref, bias_ref, o_ref, hist_ref, u_ref, act_ref,
                *, alpha, n_chunks, cw, rsub):
    rows = x_ref.shape[1]
    H = SUBLANES

    @pl.when(pl.program_id(1) == 0)
    def _():
        hist_ref[...] = jnp.zeros_like(hist_ref)

    xb = x_ref[0].astype(BF16)

    def up_proj(c, slot):
        u_ref[slot, 0:H, :] = hist_ref[c]
        u_ref[slot, H:, :] = _dot(xb, wup_ref[c])
        hist_ref[c] = u_ref[slot, rows:rows + H, :]

    def conv_gate(c, slot):
        w = cw_ref[c]
        b = cb_ref[c]
        for r in range(rows // rsub):
            base = H + r * rsub
            cu = (b + u_ref[slot, base - 2:base - 2 + rsub, :] * w[0:1, :]
                  + u_ref[slot, base - 1:base - 1 + rsub, :] * w[1:2, :]
                  + u_ref[slot, base:base + rsub, :] * w[2:3, :])
            ug = cu[:, 0:cw]
            act = (ug * jax.nn.sigmoid(ug)) * cu[:, cw:]
            act_ref[c, r * rsub:(r + 1) * rsub, :] = act.astype(BF16)

    up_proj(0, 0)
    for c in range(1, n_chunks):
        up_proj(c, c % 2)
        conv_gate(c - 1, (c - 1) % 2)
    conv_gate(n_chunks - 1, (n_chunks - 1) % 2)
    act = jnp.concatenate([act_ref[c] for c in range(n_chunks)], axis=1)
    f = _dot(act, wdn_ref[...])
    o_ref[0] = _layer_norm(alpha * x_ref[0] + f, gain_ref[...], bias_ref[...])


def _ffn(x, wup, cwt, cbs, wdn, gain, bias, rows, alpha):
    B, L, D = x.shape
    n_chunks, _, cw2 = wup.shape

    def full(a):
        return pl.BlockSpec(a.shape, lambda b, i, _n=a.ndim: (0,) * _n, pipeline_mode=pl.Buffered(1))

    rowblk = pl.BlockSpec((1, rows, D), lambda b, i: (b, i, 0))
    return pl.pallas_call(
        functools.partial(_ffn_kernel, alpha=alpha, n_chunks=n_chunks, cw=cw2 // 2, rsub=min(rows, 64)),
        grid=(B, L // rows),
        in_specs=[rowblk, full(wup), full(cwt), full(cbs), full(wdn), full(gain), full(bias)],
        out_specs=rowblk,
        out_shape=jax.ShapeDtypeStruct((B, L, D), F32),
        scratch_shapes=[pltpu.VMEM((n_chunks, SUBLANES, cw2), F32),
                        pltpu.VMEM((2, rows + SUBLANES, cw2), F32),
                        pltpu.VMEM((n_chunks, rows, cw2 // 2), BF16)],
        compiler_params=_cparams(("arbitrary", "arbitrary")),
        name="ffn_ln2",
    )(x, wup, cwt, cbs, wdn, gain, bias)


def _pad_cols(w, width):
    return jnp.pad(w, ((0, 0), (0, width - w.shape[1])))


def _ffn_chunk_width(d_ff):
    for cw in (512, 384, 256, 128):
        if d_ff % cw == 0:
            return cw
    raise ValueError(f"d_ff={d_ff} must be a multiple of {LANES}")


def _token_mixers(x, w_in, w_a2, b_a, gla_gain):
    B, L, D = x.shape
    nqk = GLA_HEADS * GLA_DK
    nv = GLA_HEADS * GLA_DV
    ndq = DSA_HEADS * DSA_HEAD_DIM
    niq = IDX_HEADS * IDX_DIM
    widths = (nqk, nqk, nv, GLA_GATE_RANK, nv, ndq, DSA_HEAD_DIM, DSA_HEAD_DIM, niq, IDX_DIM, IDX_HEADS, N_BRANCH * D)
    offs = np.concatenate([[0], np.cumsum(widths)])
    assert w_in.shape == (D, int(offs[-1]))
    col = lambda n: w_in[:, int(offs[n]):int(offs[n + 1])]
    w_gq, w_gk, w_gv, w_ga, w_gr, w_dq, w_dk, w_dv, w_iq, w_ik, w_iw, w_gate = [col(n) for n in range(12)]
    wts = (
        jnp.concatenate([w_gq, w_gk, w_gv, w_gr], axis=1).astype(BF16),
        _pad_cols(w_ga, LANES).astype(BF16),
        jnp.pad(w_a2, ((0, LANES - GLA_GATE_RANK), (0, 0))).astype(BF16),
        b_a.reshape(1, nqk).astype(F32),
        w_dq.astype(BF16),
        jnp.concatenate([w_dk, w_dv], axis=1).astype(BF16),
        w_iq.astype(BF16),
        _pad_cols(w_ik, LANES).astype(BF16),
        _pad_cols(w_iw, LANES).astype(BF16),
    )
    def lane_freqs(head_dim):
        rot = head_dim // ROPE_FRACTION
        half = rot // 2
        inv = ROPE_THETA ** (-jnp.arange(half, dtype=F32) * 2.0 / rot)
        m = np.arange(LANES) % head_dim
        return jnp.where(jnp.asarray(m < rot), inv[jnp.asarray(m % half)], 0.0)

    invf = jnp.stack([lane_freqs(DSA_HEAD_DIM), lane_freqs(IDX_DIM)]).reshape(2, 1, LANES)
    tabs = _rope_tables(L, invf)

    tq = LANES
    tk = min(L, 512)
    gla_in, g, dq, dkv, dkvt, iq, ik, iw = _in_proj(x, wts, tabs, rows=tk)
    y_gla = _gla(gla_in, g, gla_gain.reshape(1, nv).astype(F32), rows=min(L, 512))
    y_dsa = _dsa(dq, dkv, dkvt, iq, ik, iw, tq, tk)
    return y_gla, y_dsa, w_gate


def _layer(x, w_in, w_a2, b_a, gla_gain, w_branch, w_o, ln1_g, ln1_b, w_up, conv_w, conv_b, w_down, ln2_g, ln2_b,
           depth):
    B, L, D = x.shape
    alpha = (2.0 * depth) ** 0.25
    y_gla, y_dsa, w_gate = _token_mixers(x, w_in, w_a2, b_a, gla_gain)
    x1 = _merge(x, y_gla, y_dsa, w_gate.astype(BF16), w_branch.astype(BF16), w_o.astype(BF16),
                ln1_g.reshape(1, D), ln1_b.reshape(1, D), rows=min(L, 512), alpha=alpha)

    d_ff = w_down.shape[0]
    cw = _ffn_chunk_width(d_ff)
    nch = d_ff // cw
    chunked = lambda a: jnp.concatenate(
        [a[..., :d_ff].reshape(a.shape[:-1] + (nch, cw)), a[..., d_ff:].reshape(a.shape[:-1] + (nch, cw))], axis=-1)
    wup = jnp.moveaxis(chunked(w_up), 1, 0).astype(BF16)
    cwt = jnp.moveaxis(chunked(conv_w), 1, 0)
    cwt = jnp.pad(cwt, ((0, 0), (0, SUBLANES - CONV_WIDTH), (0, 0)))
    cbs = jnp.moveaxis(chunked(conv_b[None, :]), 1, 0)
    return _ffn(x1, wup, cwt, cbs, w_down.astype(BF16), ln2_g.reshape(1, D), ln2_b.reshape(1, D),
                rows=min(L, 512), alpha=alpha)


def kernel(x, w_in, w_gla_a2, b_gla_a, gla_norm_gain, w_branch, w_o, ln1_gain, ln1_bias,
           w_up, conv_w, conv_b, w_down, ln2_gain, ln2_bias):
    depth = w_in.shape[0]
    for layer in range(depth):
        x = _layer(x, w_in[layer], w_gla_a2[layer], b_gla_a[layer], gla_norm_gain[layer], w_branch[layer],
                   w_o[layer], ln1_gain[layer], ln1_bias[layer], w_up[layer], conv_w[layer], conv_b[layer],
                   w_down[layer], ln2_gain[layer], ln2_bias[layer], depth)
    return x
```

```python
import functools
import math

import numpy as np
import jax
import jax.numpy as jnp
from jax import lax
from jax.experimental import pallas as pl
from jax.experimental.pallas import tpu as pltpu

F32 = jnp.float32
BF16 = jnp.bfloat16
I32 = jnp.int32

GLA_HEADS = 4
GLA_DK = 64
GLA_DV = 128
GLA_GATE_RANK = 16
GLA_TAU = 16.0
DSA_HEADS = 8
DSA_HEAD_DIM = 64
IDX_HEADS = 8
IDX_DIM = 32
DSA_TOPK_MAX = 256
ROPE_THETA = 500000.0
ROPE_FRACTION = 4
N_BRANCH = 2
BRANCH_WIDTH = 512
CONV_WIDTH = 3
LN_EPS = 1e-5
RMS_EPS = 1e-6

LANES = 128
SUBLANES = 8
VMEM_LIMIT_BYTES = 56 * 1024 * 1024

GLA_CHUNK = 64
GLA_SUB = 16
WORD_BITS = 32
INT_MIN = -(2 ** 31)
INT_MAX = 2 ** 31 - 1
NEG_BIG = -1e30
LOG2_E = 1.4426950408889634


def _cparams(sem):
    return pltpu.CompilerParams(dimension_semantics=sem, vmem_limit_bytes=VMEM_LIMIT_BYTES)


def _log2(n):
    assert n > 0 and n & (n - 1) == 0, f"{n} must be a power of two"
    return n.bit_length() - 1


def _div_pow2(x, n):
    return lax.shift_right_logical(x, jnp.int32(_log2(n)))


def _mod_pow2(x, n):
    _log2(n)
    return x & (n - 1)


def _dot(a, b):
    return jnp.dot(a, b, preferred_element_type=F32)


def _dot_nt(a, b):
    return lax.dot_general(a, b, (((1,), (1,)), ((), ())), preferred_element_type=F32)


def _dot_tn(a, b):
    return lax.dot_general(a, b, (((0,), (0,)), ((), ())), preferred_element_type=F32)


def _rope_table_kernel(invf_ref, c_ref, sa_ref, sb_ref, *, rows, periods, halves):
    t0 = pl.program_id(0) * rows
    pos = (t0 + lax.broadcasted_iota(I32, (rows, LANES), 0)).astype(F32)
    lane = lax.broadcasted_iota(I32, (rows, LANES), 1)
    for p, (period, half) in enumerate(zip(periods, halves)):
        ang = pos * invf_ref[p]
        c = jnp.cos(ang)
        s = jnp.sin(ang)
        m = _mod_pow2(lane, period)
        is1 = m < half
        is2 = (m >= half) & (m < 2 * half)
        c_ref[p] = jnp.where(is1 | is2, c, 1.0)
        sa_ref[p] = jnp.where(is1, -s, 0.0)
        sb_ref[p] = jnp.where(is2, s, 0.0)


def _rope_tables(L, invf):
    rows = min(L, 512)
    n = invf.shape[0]
    spec = pl.BlockSpec((n, rows, LANES), lambda i: (0, i, 0))
    shp = jax.ShapeDtypeStruct((n, L, LANES), F32)
    return pl.pallas_call(
        functools.partial(_rope_table_kernel, rows=rows, periods=(DSA_HEAD_DIM, IDX_DIM),
                          halves=(DSA_HEAD_DIM // ROPE_FRACTION // 2, IDX_DIM // ROPE_FRACTION // 2)),
        grid=(L // rows,),
        in_specs=[pl.BlockSpec((n, 1, LANES), lambda i: (0, 0, 0))],
        out_specs=[spec, spec, spec],
        out_shape=[shp, shp, shp],
        compiler_params=_cparams(("arbitrary",)),
        name="rope_tables",
    )(invf)


def _apply_rope(x, c, sa, sb, half, lane_limit=None):
    outs = []
    for g in range(x.shape[1] // LANES):
        xs = x[:, g * LANES:(g + 1) * LANES]
        up = pltpu.roll(xs, LANES - half, axis=1)
        dn = pltpu.roll(xs, half, axis=1)
        y = xs * c + up * sa + dn * sb
        if lane_limit is not None:
            lane = lax.broadcasted_iota(I32, xs.shape, 1)
            y = jnp.where(lane < lane_limit, y, xs)
        outs.append(y)
    return outs[0] if len(outs) == 1 else jnp.concatenate(outs, axis=1)


def _in_proj_kernel(x_ref, wgla_ref, wga_ref, wa2_ref, ba_ref, wdq_ref, wdkv_ref, wiq_ref, wik_ref, wiw_ref,
                    c_ref, sa_ref, sb_ref,
                    gla_ref, g_ref, dq_ref, dkv_ref, dkvt_ref, iq_ref, ik_ref, iw_ref):
    xb = x_ref[0].astype(BF16)
    hg = _dot(xb, wgla_ref[...])
    nqk = GLA_HEADS * GLA_DK
    gla_ref[0, :, 0:nqk] = (hg[:, 0:nqk] * (GLA_DK ** -0.5)).astype(BF16)
    gla_ref[0, :, nqk:] = hg[:, nqk:].astype(BF16)
    a_low = _dot(xb, wga_ref[...])
    z = _dot(a_low.astype(BF16), wa2_ref[...]) + ba_ref[...]
    g_ref[0] = (jnp.minimum(z, 0.0) - jnp.log1p(jnp.exp(-jnp.abs(z)))) * (1.0 / GLA_TAU)
    c64, sa64, sb64 = c_ref[0], sa_ref[0], sb_ref[0]
    c32, sa32, sb32 = c_ref[1], sa_ref[1], sb_ref[1]
    h64 = DSA_HEAD_DIM // ROPE_FRACTION // 2
    h32 = IDX_DIM // ROPE_FRACTION // 2
    dq = _dot(xb, wdq_ref[...])
    dq_ref[0] = (_apply_rope(dq, c64, sa64, sb64, h64) * (DSA_HEAD_DIM ** -0.5 * LOG2_E)).astype(BF16)
    dkv = _apply_rope(_dot(xb, wdkv_ref[...]), c64, sa64, sb64, h64, lane_limit=DSA_HEAD_DIM)
    dkv_ref[0] = dkv.astype(BF16)
    dkvt_ref[0, 0] = dkv.T.astype(BF16)
    iq = _dot(xb, wiq_ref[...])
    iq_ref[0] = _apply_rope(iq, c32, sa32, sb32, h32).astype(BF16)
    ik = _dot(xb, wik_ref[...])
    ik_ref[0] = _apply_rope(ik, c32, sa32, sb32, h32).astype(BF16)
    iw_ref[0] = _dot(xb, wiw_ref[...]) * ((IDX_HEADS ** -0.5) * (IDX_DIM ** -0.5))


def _in_proj(x, wts, tabs, rows):
    B, L, D = x.shape
    wgla, wga, wa2, ba, wdq, wdkv, wiq, wik, wiw = wts
    c, sa, sb = tabs
    n_l = L // rows

    def full(a):
        return pl.BlockSpec(a.shape, lambda i, b, _n=a.ndim: (0,) * _n)

    def rowblk(w):
        return pl.BlockSpec((1, rows, w), lambda i, b: (b, i, 0))

    tab = pl.BlockSpec((2, rows, LANES), lambda i, b: (0, i, 0))
    widths = (wgla.shape[1], GLA_HEADS * GLA_DK, wdq.shape[1], wdkv.shape[1], wiq.shape[1], wik.shape[1], wiw.shape[1])
    dtypes = (BF16, F32, BF16, BF16, BF16, BF16, F32)
    out_specs = [rowblk(w) for w in widths]
    out_shape = [jax.ShapeDtypeStruct((B, L, w), dt) for w, dt in zip(widths, dtypes)]
    out_specs.insert(4, pl.BlockSpec((1, 1, wdkv.shape[1], rows), lambda i, b: (b, i, 0, 0)))
    out_shape.insert(4, jax.ShapeDtypeStruct((B, n_l, wdkv.shape[1], rows), BF16))
    return pl.pallas_call(
        _in_proj_kernel,
        grid=(n_l, B),
        in_specs=[rowblk(D)] + [full(w) for w in wts] + [tab, tab, tab],
        out_specs=out_specs,
        out_shape=out_shape,
        compiler_params=_cparams(("arbitrary", "arbitrary")),
        name="in_proj",
    )(x, *wts, c, sa, sb)


def _gla_kernel(qkvr_ref, g_ref, gain_ref, tril_ref, expand_ref, y_ref, st_ref, *, rows):
    C, S = GLA_CHUNK, GLA_SUB
    H, DK, DV = GLA_HEADS, GLA_DK, GLA_DV
    HK, HV = H * DK, H * DV

    @pl.when(pl.program_id(1) == 0)
    def _():
        st_ref[...] = jnp.zeros_like(st_ref)

    ri = lax.broadcasted_iota(I32, (C, HK), 0)
    li = lax.broadcasted_iota(I32, (C, HK), 1)
    head_of_lane = _div_pow2(li, DK)
    sr = _div_pow2(lax.broadcasted_iota(I32, (HV, HK), 0), DV)
    scol = _div_pow2(lax.broadcasted_iota(I32, (HV, HK), 1), DK)
    st_mask = sr == scol
    tril = tril_ref[...]
    expand = expand_ref[...]
    gain = gain_ref[...]

    def chunk(ci, carry):
        r0 = pl.multiple_of(ci * C, C)
        blk = qkvr_ref[0, pl.ds(r0, C), :]
        q = blk[:, 0:HK].astype(F32)
        k = blk[:, HK:2 * HK].astype(F32)
        v = blk[:, 2 * HK:2 * HK + HV].astype(F32)
        r = blk[:, 2 * HK + HV:].astype(F32)
        g = g_ref[0, pl.ds(r0, C), :]
        g1 = g.astype(BF16)
        rem = g - g1.astype(F32)
        g2 = rem.astype(BF16)
        g3 = (rem - g2.astype(F32)).astype(BF16)
        b = _dot(tril, g1) + _dot(tril, g2) + _dot(tril, g3)
        b_last = b[C - 1:C, :]
        st = st_ref[...]
        o = _dot_nt((q * jnp.exp(b)).astype(BF16), st.astype(BF16))
        p_acc = jnp.zeros((H * C, C), F32)
        for I in range(1, C // S):
            beta = b[I * S - 1:I * S, :]
            in_I = (ri >= I * S) & (ri < (I + 1) * S)
            qI = jnp.where(in_I, q * jnp.exp(jnp.minimum(b - beta, 0.0)), 0.0)
            kI = jnp.where(ri < I * S, k * jnp.exp(jnp.minimum(beta - b, 0.0)), 0.0)
            q_st = jnp.concatenate([jnp.where(head_of_lane == h, qI, 0.0) for h in range(H)], axis=0)
            p_acc = p_acc + _dot_nt(q_st.astype(BF16), kI.astype(BF16))
        p_b = p_acc.astype(BF16)
        vb = v.astype(BF16)
        o_off = jnp.concatenate(
            [_dot(p_b[h * C:(h + 1) * C, :], vb[:, h * DV:(h + 1) * DV]) for h in range(H)], axis=1)
        o = o + o_off
        GRP = 4
        for d0 in range(0, S, GRP):
            ts = []
            for d in range(d0, d0 + GRP):
                if d == 0:
                    ts.append(q * k)
                else:
                    kd = pltpu.roll(k, d, axis=0)
                    bd = pltpu.roll(b, d, axis=0)
                    valid = _mod_pow2(ri, S) >= d
                    ts.append(q * kd * jnp.exp(jnp.where(valid, jnp.minimum(b - bd, 0.0), -jnp.inf)))
            s_all = _dot(jnp.concatenate(ts, axis=0).astype(BF16), expand)
            for n, d in enumerate(range(d0, d0 + GRP)):
                vd = v if d == 0 else pltpu.roll(v, d, axis=0)
                o = o + s_all[n * C:(n + 1) * C, :] * vd
        k_end = k * jnp.exp(b_last - b)
        upd = _dot_tn(vb, k_end.astype(BF16))
        st_ref[...] = st * jnp.exp(b_last) + jnp.where(st_mask, upd, 0.0)
        outs = []
        for h in range(H):
            oh = o[:, h * DV:(h + 1) * DV]
            ms = jnp.mean(oh * oh, axis=-1, keepdims=True)
            outs.append(oh * lax.rsqrt(ms + RMS_EPS))
        on = jnp.concatenate(outs, axis=1) * gain
        y = on * (r * jax.nn.sigmoid(r))
        y_ref[0, pl.ds(r0, C), :] = y.astype(y_ref.dtype)
        return carry

    lax.fori_loop(0, rows // C, chunk, 0)


def _gla(qkvr, g, gain, rows):
    B, L, W = qkvr.shape
    HK, HV = GLA_HEADS * GLA_DK, GLA_HEADS * GLA_DV
    C = GLA_CHUNK
    tril = jnp.asarray(np.tril(np.ones((C, C), np.float32)), BF16)
    expand = jnp.asarray(np.kron(np.eye(GLA_HEADS, dtype=np.float32), np.ones((GLA_DK, GLA_DV), np.float32)), BF16)
    return pl.pallas_call(
        functools.partial(_gla_kernel, rows=rows),
        grid=(B, L // rows),
        in_specs=[pl.BlockSpec((1, rows, W), lambda b, i: (b, i, 0)),
                  pl.BlockSpec((1, rows, HK), lambda b, i: (b, i, 0)),
                  pl.BlockSpec((1, HV), lambda b, i: (0, 0)),
                  pl.BlockSpec((C, C), lambda b, i: (0, 0)),
                  pl.BlockSpec((HK, HV), lambda b, i: (0, 0))],
        out_specs=pl.BlockSpec((1, rows, HV), lambda b, i: (b, i, 0)),
        out_shape=jax.ShapeDtypeStruct((B, L, HV), BF16),
        scratch_shapes=[pltpu.VMEM((HV, HK), F32)],
        compiler_params=_cparams(("arbitrary", "arbitrary")),
        name="gla",
    )(qkvr, g, gain, tril, expand)


def _bit_transpose32(words):
    a = list(words)
    assert len(a) == WORD_BITS
    j, m = WORD_BITS // 2, 0x0000FFFF
    while j:
        m_i32 = m - (1 << 32) if m >= (1 << 31) else m
        for k in range(WORD_BITS):
            if k & j == 0:
                t = (a[k] ^ lax.shift_right_logical(a[k + j], jnp.int32(j))) & m_i32
                a[k] = a[k] ^ t
                a[k + j] = a[k + j] ^ lax.shift_left(t, jnp.int32(j))
        j >>= 1
        m = (m ^ (m << j)) & 0xFFFFFFFF
    return a


def _dsa_kernel(dq_ref, kv_ref, kvt_ref, iq_ref, ik_ref, iw_ref, o_ref,
                key_ref, plane_ref, cand_ref, qt_ref, qit_ref, wt_ref, m_ref, l_ref, acc_ref, s_ref, smax_ref, p_ref,
                *, tq, tk, top_k):
    NH, DH = DSA_HEADS, DSA_HEAD_DIM
    assert tq == LANES
    i = pl.program_id(1)
    t0 = i * tq
    nkb = (t0 + tq + tk - 1) // tk

    qt_ref[DH:, :] = jnp.zeros((DH, NH * tq), BF16)
    for p in range(NH // 2):
        pair_t = dq_ref[0, :, p * LANES:(p + 1) * LANES].astype(F32).T
        qt_ref[0:DH, (2 * p) * tq:(2 * p + 1) * tq] = pair_t[0:DH].astype(BF16)
        qt_ref[0:DH, (2 * p + 1) * tq:(2 * p + 2) * tq] = pair_t[DH:].astype(BF16)
    qit_ref[...] = jnp.zeros(qit_ref.shape, BF16)
    heads_per_group = LANES // IDX_DIM
    for g in range(IDX_HEADS // heads_per_group):
        grp_t = iq_ref[0, :, g * LANES:(g + 1) * LANES].astype(F32).T
        for u in range(heads_per_group):
            h = g * heads_per_group + u
            qit_ref[0:IDX_DIM, h * tq:(h + 1) * tq] = grp_t[u * IDX_DIM:(u + 1) * IDX_DIM].astype(BF16)
    wt_ref[...] = iw_ref[0].T[0:SUBLANES]

    q_pos = t0 + lax.broadcasted_iota(I32, (tk, tq), 1)
    k_off = lax.broadcasted_iota(I32, (tk, tq), 0)
    q_pos8 = t0 + lax.broadcasted_iota(I32, (SUBLANES, tq), 1)
    sub8 = lax.broadcasted_iota(I32, (SUBLANES, tq), 0)
    groups_per_block = tk // (WORD_BITS * SUBLANES)

    @pl.when((pl.program_id(0) == 0) & (i == 0))
    def _():
        plane_ref[...] = jnp.zeros(plane_ref.shape, I32)

    def score_block(j, carry):
        k0 = pl.multiple_of(j * tk, tk)
        ikb = ik_ref[0, pl.ds(k0, tk), :]
        sc = jnp.zeros((tk, tq), F32)
        for c in range(IDX_HEADS // 2):
            lg = _dot(ikb, qit_ref[:, (2 * c) * tq:(2 * c + 2) * tq])
            for u in range(2):
                h = 2 * c + u
                sc = sc + jnp.maximum(lg[:, u * tq:(u + 1) * tq], 0.0) * wt_ref[h:h + 1, :]
        sc = sc + 0.0
        bits = pltpu.bitcast(sc, I32)
        key = bits ^ ((bits >> 31) & 0x7FFFFFFF)
        key = jnp.where(k0 + k_off <= q_pos, key, INT_MIN)
        key_ref[j] = key
        ukey = key ^ INT_MIN
        for g in range(groups_per_block):
            words = [ukey[(g * WORD_BITS + w) * SUBLANES:(g * WORD_BITS + w + 1) * SUBLANES, :]
                     for w in range(WORD_BITS)]
            planes = _bit_transpose32(words)
            r0 = pl.multiple_of((j * groups_per_block + g) * SUBLANES, SUBLANES)
            for p in range(WORD_BITS):
                plane_ref[p, pl.ds(r0, SUBLANES), :] = planes[p]
            n_adm = ((q_pos8 - (k0 + g * WORD_BITS * SUBLANES) - sub8) >> _log2(SUBLANES)) + 1
            n_adm = jnp.clip(n_adm, 0, WORD_BITS)
            cand_ref[pl.ds(r0, SUBLANES), :] = jnp.where(
                n_adm == 0, 0, lax.shift_left(jnp.int32(-1), WORD_BITS - jnp.maximum(n_adm, 1)))
        return carry

    cand_ref[...] = jnp.zeros(cand_ref.shape, I32)
    lax.fori_loop(0, nkb, score_block, 0)

    def popcount_rows(words):
        pc = lax.population_count(words)
        parts = [pc[r * SUBLANES:(r + 1) * SUBLANES, :] for r in range(words.shape[0] // SUBLANES)]
        while len(parts) > 1:
            parts = [a + b for a, b in zip(parts[0::2], parts[1::2])]
        return jnp.sum(parts[0].astype(F32), axis=0, keepdims=True)

    def select_bit(p, carry):
        thr_u, rank = carry
        plane = plane_ref[p]
        cand = cand_ref[...]
        ones = cand & plane
        n_ones = popcount_rows(ones)
        take = n_ones >= rank
        thr_u = thr_u | jnp.where(take, lax.shift_left(jnp.int32(1), WORD_BITS - 1 - p), 0)
        rank = jnp.where(take, rank, rank - n_ones)
        cand_ref[...] = cand & (plane ^ jnp.where(take, 0, -1))
        return thr_u, rank

    thr_u, rank = lax.fori_loop(0, WORD_BITS, select_bit,
                                (jnp.zeros((1, tq), I32), jnp.full((1, tq), top_k, F32)))
    thr = thr_u ^ INT_MIN
    n_tied = popcount_rows(cand_ref[...])

    def count_rows(hit_fn):
        acc_rows = 8 * SUBLANES

        def body(j, cnt):
            hit = hit_fn(j)
            return cnt + jnp.sum(hit.reshape(tk // acc_rows, acc_rows, tq), axis=0)

        cnt = lax.fori_loop(0, nkb, body, jnp.zeros((acc_rows, tq), F32))
        return jnp.sum(cnt, axis=0, keepdims=True)

    thr = jnp.maximum(thr, INT_MIN + 1)
    excess = jnp.max(jnp.where(n_tied > rank, 1.0, 0.0))

    @pl.when(excess > 0.0)
    def _():
        need = rank
        nbits = max(1, int(math.ceil(math.log2(key_ref.shape[0] * tk))))

        def count_eq_le(idx_lim):
            return count_rows(
                lambda j: jnp.where((key_ref[j] == thr) & (j * tk + k_off <= idx_lim), 1.0, 0.0))

        def idx_bit(it, lo):
            trial = lo + lax.shift_left(jnp.int32(1), nbits - 1 - it)
            return jnp.where(count_eq_le(trial - 1) < need, trial, lo)

        cut = lax.fori_loop(0, nbits, idx_bit, jnp.zeros((1, tq), I32))

        def drop(j, carry):
            key = key_ref[j]
            key_ref[j] = jnp.where((key == thr) & (j * tk + k_off > cut), INT_MIN, key)
            return carry

        lax.fori_loop(0, nkb, drop, 0)

    m_ref[...] = jnp.full(m_ref.shape, NEG_BIG, F32)
    l_ref[...] = jnp.zeros(l_ref.shape, F32)
    acc_ref[...] = jnp.zeros(acc_ref.shape, F32)
    cw = 2 * tq
    n_col = NH * tq
    rc = 4 * SUBLANES

    def logits(j, slot, live):
        k0 = pl.multiple_of(j * tk, tk)
        kvb = kv_ref[0, pl.ds(k0, tk), :]
        floor = jnp.where(live, jnp.int32(INT_MIN), jnp.int32(INT_MAX))
        bias = jnp.where(key_ref[j] >= jnp.maximum(thr, floor), 0.0, NEG_BIG)
        bias2 = jnp.concatenate([bias, bias], axis=1)
        for c in range(NH // 2):
            cols = slice(c * cw, (c + 1) * cw)
            s = _dot(kvb, qt_ref[:, cols]) + bias2
            s_ref[slot, :, cols] = s
            smax_ref[slot, :, cols] = jnp.max(s, axis=0, keepdims=True)

    def softmax_values(j, slot):
        m_old = m_ref[...]
        m_new = jnp.maximum(m_old, smax_ref[slot])
        alpha = jnp.exp2(m_old - m_new)
        m_ref[...] = m_new
        vt = kvt_ref[0, j, DH:, :]
        for c in range(NH // 2):
            cols = slice(c * cw, (c + 1) * cw)
            m_rows = jnp.broadcast_to(m_new[:, cols], (rc, cw))
            psum = jnp.zeros((SUBLANES, cw), F32)
            for r in range(tk // rc):
                p = jnp.exp2(s_ref[slot, r * rc:(r + 1) * rc, cols] - m_rows)
                p_ref[r * rc:(r + 1) * rc, cols] = p.astype(BF16)
                for u in range(rc // SUBLANES):
                    psum = psum + p[u * SUBLANES:(u + 1) * SUBLANES]
            l_ref[:, cols] = l_ref[:, cols] * alpha[:, cols] + jnp.sum(psum, axis=0, keepdims=True)
            acc_ref[:, cols] = acc_ref[:, cols] * alpha[:, cols] + _dot(vt, p_ref[:, cols])

    last = nkb - 1
    logits(0, 0, True)

    def attend_pair(jj, carry):
        j0 = 2 * jj
        logits(jnp.minimum(j0 + 1, last), 1, j0 + 1 <= last)
        softmax_values(j0, 0)
        logits(jnp.minimum(j0 + 2, last), 0, j0 + 2 <= last)
        softmax_values(jnp.minimum(j0 + 1, last), 1)
        return carry

    lax.fori_loop(0, (nkb + 1) // 2, attend_pair, 0)

    for p in range(NH // 2):
        pair_t = jnp.concatenate(
            [acc_ref[:, (2 * p + u) * tq:(2 * p + u + 1) * tq] / l_ref[:, (2 * p + u) * tq:(2 * p + u + 1) * tq]
             for u in range(2)], axis=0)
        o_ref[0, :, p * LANES:(p + 1) * LANES] = pair_t.T.astype(o_ref.dtype)


def _dsa(dq, dkv, dkvt, iq, ik, iw, tq, tk):
    B, L, _ = dq.shape
    top_k = min(DSA_TOPK_MAX, L // 4)
    nkb_max = L // tk
    assert dkvt.shape == (B, nkb_max, dkv.shape[2], tk)
    return pl.pallas_call(
        functools.partial(_dsa_kernel, tq=tq, tk=tk, top_k=float(top_k)),
        grid=(B, L // tq),
        in_specs=[pl.BlockSpec((1, tq, dq.shape[2]), lambda b, i: (b, i, 0)),
                  pl.BlockSpec((1, L, dkv.shape[2]), lambda b, i: (b, 0, 0)),
                  pl.BlockSpec((1,) + dkvt.shape[1:], lambda b, i: (b, 0, 0, 0)),
                  pl.BlockSpec((1, tq, iq.shape[2]), lambda b, i: (b, i, 0)),
                  pl.BlockSpec((1, L, ik.shape[2]), lambda b, i: (b, 0, 0)),
                  pl.BlockSpec((1, tq, iw.shape[2]), lambda b, i: (b, i, 0))],
        out_specs=pl.BlockSpec((1, tq, dq.shape[2]), lambda b, i: (b, i, 0)),
        out_shape=jax.ShapeDtypeStruct(dq.shape, BF16),
        scratch_shapes=[pltpu.VMEM((nkb_max, tk, tq), I32),
                        pltpu.VMEM((WORD_BITS, L // WORD_BITS, tq), I32),
                        pltpu.VMEM((L // WORD_BITS, tq), I32),
                        pltpu.VMEM((2 * DSA_HEAD_DIM, DSA_HEADS * tq), BF16),
                        pltpu.VMEM((LANES, IDX_HEADS * tq), BF16),
                        pltpu.VMEM((SUBLANES, tq), F32),
                        pltpu.VMEM((1, DSA_HEADS * tq), F32),
                        pltpu.VMEM((1, DSA_HEADS * tq), F32),
                        pltpu.VMEM((DSA_HEAD_DIM, DSA_HEADS * tq), F32),
                        pltpu.VMEM((2, tk, DSA_HEADS * tq), F32),
                        pltpu.VMEM((2, 1, DSA_HEADS * tq), F32),
                        pltpu.VMEM((tk, DSA_HEADS * tq), BF16)],
        compiler_params=_cparams(("arbitrary", "arbitrary")),
        name="dsa",
    )(dq, dkv, dkvt, iq, ik, iw)


def _layer_norm(z, gain, bias):
    mu = jnp.mean(z, axis=-1, keepdims=True)
    zc = z - mu
    var = jnp.mean(zc * zc, axis=-1, keepdims=True)
    return zc * lax.rsqrt(var + LN_EPS) * gain + bias


def _merge_kernel(x_ref, yg_ref, yd_ref, wgate_ref, wbr_ref, wo_ref, gain_ref, bias_ref, o_ref, *, alpha):
    x = x_ref[0]
    D = x.shape[1]
    gates = jax.nn.sigmoid(_dot(x.astype(BF16), wgate_ref[...]))
    pg = _dot(yg_ref[0], wbr_ref[0])
    pd = _dot(yd_ref[0], wbr_ref[1])
    m = gates[:, 0:D] * pg + gates[:, D:] * pd
    mixed = _dot(m.astype(BF16), wo_ref[...])
    o_ref[0] = _layer_norm(alpha * x + mixed, gain_ref[...], bias_ref[...])


def _merge(x, yg, yd, wgate, wbr, wo, gain, bias, rows, alpha):
    B, L, D = x.shape

    def full(a):
        return pl.BlockSpec(a.shape, lambda b, i, _n=a.ndim: (0,) * _n)

    def rowblk(w):
        return pl.BlockSpec((1, rows, w), lambda b, i: (b, i, 0))

    return pl.pallas_call(
        functools.partial(_merge_kernel, alpha=alpha),
        grid=(B, L // rows),
        in_specs=[rowblk(D), rowblk(yg.shape[2]), rowblk(yd.shape[2]),
                  full(wgate), full(wbr), full(wo), full(gain), full(bias)],
        out_specs=rowblk(D),
        out_shape=jax.ShapeDtypeStruct((B, L, D), F32),
        compiler_params=_cparams(("arbitrary", "arbitrary")),
        name="merge_ln1",
    )(x, yg, yd, wgate, wbr, wo, gain, bias)


def _ffn_kernel(x_ref, wup_ref, cw_ref, cb_ref, wdn_ref, gain_ref, bias_ref, o_ref, hist_ref, u_ref, act_ref,
                *, alpha, n_chunks, cw, rsub):
    rows = x_ref.shape[1]
    H = SUBLANES

    @pl.when(pl.program_id(1) == 0)
    def _():
        hist_ref[...] = jnp.zeros_like(hist_ref)

    xb = x_ref[0].astype(BF16)

    def up_proj(c, slot):
        u_ref[slot, 0:H, :] = hist_ref[c]
        u_ref[slot, H:, :] = _dot(xb, wup_ref[c])
        hist_ref[c] = u_ref[slot, rows:rows + H, :]

    def conv_gate(c, slot):
        w = cw_ref[c]
        b = cb_ref[c]
        for r in range(rows // rsub):
            base = H + r * rsub
            cu = (b + u_ref[slot, base - 2:base - 2 + rsub, :] * w[0:1, :]
                  + u_ref[slot, base - 1:base - 1 + rsub, :] * w[1:2, :]
                  + u_ref[slot, base:base + rsub, :] * w[2:3, :])
            ug = cu[:, 0:cw]
            act = (ug * jax.nn.sigmoid(ug)) * cu[:, cw:]
            act_ref[c, r * rsub:(r + 1) * rsub, :] = act.astype(BF16)

    up_proj(0, 0)
    for c in range(1, n_chunks):
        up_proj(c, c % 2)
        conv_gate(c - 1, (c - 1) % 2)
    conv_gate(n_chunks - 1, (n_chunks - 1) % 2)
    act = jnp.concatenate([act_ref[c] for c in range(n_chunks)], axis=1)
    f = _dot(act, wdn_ref[...])
    o_ref[0] = _layer_norm(alpha * x_ref[0] + f, gain_ref[...], bias_ref[...])


def _ffn(x, wup, cwt, cbs, wdn, gain, bias, rows, alpha):
    B, L, D = x.shape
    n_chunks, _, cw2 = wup.shape

    def full(a):
        return pl.BlockSpec(a.shape, lambda b, i, _n=a.ndim: (0,) * _n, pipeline_mode=pl.Buffered(1))

    rowblk = pl.BlockSpec((1, rows, D), lambda b, i: (b, i, 0))
    return pl.pallas_call(
        functools.partial(_ffn_kernel, alpha=alpha, n_chunks=n_chunks, cw=cw2 // 2, rsub=min(rows, 64)),
        grid=(B, L // rows),
        in_specs=[rowblk, full(wup), full(cwt), full(cbs), full(wdn), full(gain), full(bias)],
        out_specs=rowblk,
        out_shape=jax.ShapeDtypeStruct((B, L, D), F32),
        scratch_shapes=[pltpu.VMEM((n_chunks, SUBLANES, cw2), F32),
                        pltpu.VMEM((2, rows + SUBLANES, cw2), F32),
                        pltpu.VMEM((n_chunks, rows, cw2 // 2), BF16)],
        compiler_params=_cparams(("arbitrary", "arbitrary")),
        name="ffn_ln2",
    )(x, wup, cwt, cbs, wdn, gain, bias)


def _pad_cols(w, width):
    return jnp.pad(w, ((0, 0), (0, width - w.shape[1])))


def _ffn_chunk_width(d_ff):
    for cw in (512, 384, 256, 128):
        if d_ff % cw == 0:
            return cw
    raise ValueError(f"d_ff={d_ff} must be a multiple of {LANES}")


def _token_mixers(x, w_in, w_a2, b_a, gla_gain):
    B, L, D = x.shape
    nqk = GLA_HEADS * GLA_DK
    nv = GLA_HEADS * GLA_DV
    ndq = DSA_HEADS * DSA_HEAD_DIM
    niq = IDX_HEADS * IDX_DIM
    widths = (nqk, nqk, nv, GLA_GATE_RANK, nv, ndq, DSA_HEAD_DIM, DSA_HEAD_DIM, niq, IDX_DIM, IDX_HEADS, N_BRANCH * D)
    offs = np.concatenate([[0], np.cumsum(widths)])
    assert w_in.shape == (D, int(offs[-1]))
    col = lambda n: w_in[:, int(offs[n]):int(offs[n + 1])]
    w_gq, w_gk, w_gv, w_ga, w_gr, w_dq, w_dk, w_dv, w_iq, w_ik, w_iw, w_gate = [col(n) for n in range(12)]
    wts = (
        jnp.concatenate([w_gq, w_gk, w_gv, w_gr], axis=1).astype(BF16),
        _pad_cols(w_ga, LANES).astype(BF16),
        jnp.pad(w_a2, ((0, LANES - GLA_GATE_RANK), (0, 0))).astype(BF16),
        b_a.reshape(1, nqk).astype(F32),
        w_dq.astype(BF16),
        jnp.concatenate([w_dk, w_dv], axis=1).astype(BF16),
        w_iq.astype(BF16),
        _pad_cols(w_ik, LANES).astype(BF16),
        _pad_cols(w_iw, LANES).astype(BF16),
    )
    def lane_freqs(head_dim):
        rot = head_dim // ROPE_FRACTION
        half = rot // 2
        inv = ROPE_THETA ** (-jnp.arange(half, dtype=F32) * 2.0 / rot)
        m = np.arange(LANES) % head_dim
        return jnp.where(jnp.asarray(m < rot), inv[jnp.asarray(m % half)], 0.0)

    invf = jnp.stack([lane_freqs(DSA_HEAD_DIM), lane_freqs(IDX_DIM)]).reshape(2, 1, LANES)
    tabs = _rope_tables(L, invf)

    tq = LANES
    tk = min(L, 512)
    gla_in, g, dq, dkv, dkvt, iq, ik, iw = _in_proj(x, wts, tabs, rows=tk)
    y_gla = _gla(gla_in, g, gla_gain.reshape(1, nv).astype(F32), rows=min(L, 512))
    y_dsa = _dsa(dq, dkv, dkvt, iq, ik, iw, tq, tk)
    return y_gla, y_dsa, w_gate


def _layer(x, w_in, w_a2, b_a, gla_gain, w_branch, w_o, ln1_g, ln1_b, w_up, conv_w, conv_b, w_down, ln2_g, ln2_b,
           depth):
    B, L, D = x.shape
    alpha = (2.0 * depth) ** 0.25
    y_gla, y_dsa, w_gate = _token_mixers(x, w_in, w_a2, b_a, gla_gain)
    x1 = _merge(x, y_gla, y_dsa, w_gate.astype(BF16), w_branch.astype(BF16), w_o.astype(BF16),
                ln1_g.reshape(1, D), ln1_b.reshape(1, D), rows=min(L, 512), alpha=alpha)

    d_ff = w_down.shape[0]
    cw = _ffn_chunk_width(d_ff)
    nch = d_ff // cw
    chunked = lambda a: jnp.concatenate(
        [a[..., :d_ff].reshape(a.shape[:-1] + (nch, cw)), a[..., d_ff:].reshape(a.shape[:-1] + (nch, cw))], axis=-1)
    wup = jnp.moveaxis(chunked(w_up), 1, 0).astype(BF16)
    cwt = jnp.moveaxis(chunked(conv_w), 1, 0)
    cwt = jnp.pad(cwt, ((0, 0), (0, SUBLANES - CONV_WIDTH), (0, 0)))
    cbs = jnp.moveaxis(chunked(conv_b[None, :]), 1, 0)
    return _ffn(x1, wup, cwt, cbs, w_down.astype(BF16), ln2_g.reshape(1, D), ln2_b.reshape(1, D),
                rows=min(L, 512), alpha=alpha)


def kernel(x, w_in, w_gla_a2, b_gla_a, gla_norm_gain, w_branch, w_o, ln1_gain, ln1_bias,
           w_up, conv_w, conv_b, w_down, ln2_gain, ln2_bias):
    depth = w_in.shape[0]
    for layer in range(depth):
        x = _layer(x, w_in[layer], w_gla_a2[layer], b_gla_a[layer], gla_norm_gain[layer], w_branch[layer],
                   w_o[layer], ln1_gain[layer], ln1_bias[layer], w_up[layer], conv_w[layer], conv_b[layer],
                   w_down[layer], ln2_gain[layer], ln2_bias[layer], depth)
    return x
```

```python
import functools
import math

import numpy as np
import jax
import jax.numpy as jnp
from jax import lax
from jax.experimental import pallas as pl
from jax.experimental.pallas import tpu as pltpu

F32 = jnp.float32
BF16 = jnp.bfloat16
I32 = jnp.int32

GLA_HEADS = 4
GLA_DK = 64
GLA_DV = 128
GLA_GATE_RANK = 16
GLA_TAU = 16.0
DSA_HEADS = 8
DSA_HEAD_DIM = 64
IDX_HEADS = 8
IDX_DIM = 32
DSA_TOPK_MAX = 256
ROPE_THETA = 500000.0
ROPE_FRACTION = 4
N_BRANCH = 2
BRANCH_WIDTH = 512
CONV_WIDTH = 3
LN_EPS = 1e-5
RMS_EPS = 1e-6

LANES = 128
SUBLANES = 8
VMEM_LIMIT_BYTES = 56 * 1024 * 1024

GLA_CHUNK = 64
GLA_SUB = 16
WORD_BITS = 32
INT_MIN = -(2 ** 31)
INT_MAX = 2 ** 31 - 1
NEG_BIG = -1e30
LOG2_E = 1.4426950408889634


def _cparams(sem):
    return pltpu.CompilerParams(dimension_semantics=sem, vmem_limit_bytes=VMEM_LIMIT_BYTES)


def _log2(n):
    assert n > 0 and n & (n - 1) == 0, f"{n} must be a power of two"
    return n.bit_length() - 1


def _div_pow2(x, n):
    return lax.shift_right_logical(x, jnp.int32(_log2(n)))


def _mod_pow2(x, n):
    _log2(n)
    return x & (n - 1)


def _dot(a, b):
    return jnp.dot(a, b, preferred_element_type=F32)


def _dot_nt(a, b):
    return lax.dot_general(a, b, (((1,), (1,)), ((), ())), preferred_element_type=F32)


def _dot_tn(a, b):
    return lax.dot_general(a, b, (((0,), (0,)), ((), ())), preferred_element_type=F32)


def _rope_table_kernel(invf_ref, c_ref, sa_ref, sb_ref, *, rows, periods, halves):
    t0 = pl.program_id(0) * rows
    pos = (t0 + lax.broadcasted_iota(I32, (rows, LANES), 0)).astype(F32)
    lane = lax.broadcasted_iota(I32, (rows, LANES), 1)
    for p, (period, half) in enumerate(zip(periods, halves)):
        ang = pos * invf_ref[p]
        c = jnp.cos(ang)
        s = jnp.sin(ang)
        m = _mod_pow2(lane, period)
        is1 = m < half
        is2 = (m >= half) & (m < 2 * half)
        c_ref[p] = jnp.where(is1 | is2, c, 1.0)
        sa_ref[p] = jnp.where(is1, -s, 0.0)
        sb_ref[p] = jnp.where(is2, s, 0.0)


def _rope_tables(L, invf):
    rows = min(L, 512)
    n = invf.shape[0]
    spec = pl.BlockSpec((n, rows, LANES), lambda i: (0, i, 0))
    shp = jax.ShapeDtypeStruct((n, L, LANES), F32)
    return pl.pallas_call(
        functools.partial(_rope_table_kernel, rows=rows, periods=(DSA_HEAD_DIM, IDX_DIM),
                          halves=(DSA_HEAD_DIM // ROPE_FRACTION // 2, IDX_DIM // ROPE_FRACTION // 2)),
        grid=(L // rows,),
        in_specs=[pl.BlockSpec((n, 1, LANES), lambda i: (0, 0, 0))],
        out_specs=[spec, spec, spec],
        out_shape=[shp, shp, shp],
        compiler_params=_cparams(("arbitrary",)),
        name="rope_tables",
    )(invf)


def _apply_rope(x, c, sa, sb, half, lane_limit=None):
    outs = []
    for g in range(x.shape[1] // LANES):
        xs = x[:, g * LANES:(g + 1) * LANES]
        up = pltpu.roll(xs, LANES - half, axis=1)
        dn = pltpu.roll(xs, half, axis=1)
        y = xs * c + up * sa + dn * sb
        if lane_limit is not None:
            lane = lax.broadcasted_iota(I32, xs.shape, 1)
            y = jnp.where(lane < lane_limit, y, xs)
        outs.append(y)
    return outs[0] if len(outs) == 1 else jnp.concatenate(outs, axis=1)


def _in_proj_kernel(x_ref, wgla_ref, wga_ref, wa2_ref, ba_ref, wdq_ref, wdkv_ref, wiq_ref, wik_ref, wiw_ref,
                    c_ref, sa_ref, sb_ref,
                    gla_ref, g_ref, dq_ref, dkv_ref, dkvt_ref, iq_ref, ik_ref, iw_ref):
    xb = x_ref[0].astype(BF16)
    hg = _dot(xb, wgla_ref[...])
    nqk = GLA_HEADS * GLA_DK
    gla_ref[0, :, 0:nqk] = (hg[:, 0:nqk] * (GLA_DK ** -0.5)).astype(BF16)
    gla_ref[0, :, nqk:] = hg[:, nqk:].astype(BF16)
    a_low = _dot(xb, wga_ref[...])
    z = _dot(a_low.astype(BF16), wa2_ref[...]) + ba_ref[...]
    g_ref[0] = (jnp.minimum(z, 0.0) - jnp.log1p(jnp.exp(-jnp.abs(z)))) * (1.0 / GLA_TAU)
    c64, sa64, sb64 = c_ref[0], sa_ref[0], sb_ref[0]
    c32, sa32, sb32 = c_ref[1], sa_ref[1], sb_ref[1]
    h64 = DSA_HEAD_DIM // ROPE_FRACTION // 2
    h32 = IDX_DIM // ROPE_FRACTION // 2
    dq = _dot(xb, wdq_ref[...])
    dq_ref[0] = (_apply_rope(dq, c64, sa64, sb64, h64) * (DSA_HEAD_DIM ** -0.5 * LOG2_E)).astype(BF16)
    dkv = _apply_rope(_dot(xb, wdkv_ref[...]), c64, sa64, sb64, h64, lane_limit=DSA_HEAD_DIM)
    dkv_ref[0] = dkv.astype(BF16)
    dkvt_ref[0, 0] = dkv.T.astype(BF16)
    iq = _dot(xb, wiq_ref[...])
    iq_ref[0] = _apply_rope(iq, c32, sa32, sb32, h32).astype(BF16)
    ik = _dot(xb, wik_ref[...])
    ik_ref[0] = _apply_rope(ik, c32, sa32, sb32, h32).astype(BF16)
    iw_ref[0] = _dot(xb, wiw_ref[...]) * ((IDX_HEADS ** -0.5) * (IDX_DIM ** -0.5))


def _in_proj(x, wts, tabs, rows):
    B, L, D = x.shape
    wgla, wga, wa2, ba, wdq, wdkv, wiq, wik, wiw = wts
    c, sa, sb = tabs
    n_l = L // rows

    def full(a):
        return pl.BlockSpec(a.shape, lambda i, b, _n=a.ndim: (0,) * _n)

    def rowblk(w):
        return pl.BlockSpec((1, rows, w), lambda i, b: (b, i, 0))

    tab = pl.BlockSpec((2, rows, LANES), lambda i, b: (0, i, 0))
    widths = (wgla.shape[1], GLA_HEADS * GLA_DK, wdq.shape[1], wdkv.shape[1], wiq.shape[1], wik.shape[1], wiw.shape[1])
    dtypes = (BF16, F32, BF16, BF16, BF16, BF16, F32)
    out_specs = [rowblk(w) for w in widths]
    out_shape = [jax.ShapeDtypeStruct((B, L, w), dt) for w, dt in zip(widths, dtypes)]
    out_specs.insert(4, pl.BlockSpec((1, 1, wdkv.shape[1], rows), lambda i, b: (b, i, 0, 0)))
    out_shape.insert(4, jax.ShapeDtypeStruct((B, n_l, wdkv.shape[1], rows), BF16))
    return pl.pallas_call(
        _in_proj_kernel,
        grid=(n_l, B),
        in_specs=[rowblk(D)] + [full(w) for w in wts] + [tab, tab, tab],
        out_specs=out_specs,
        out_shape=out_shape,
        compiler_params=_cparams(("arbitrary", "arbitrary")),
        name="in_proj",
    )(x, *wts, c, sa, sb)


def _gla_kernel(qkvr_ref, g_ref, gain_ref, tril_ref, expand_ref, y_ref, st_ref, *, rows):
    C, S = GLA_CHUNK, GLA_SUB
    H, DK, DV = GLA_HEADS, GLA_DK, GLA_DV
    HK, HV = H * DK, H * DV

    @pl.when(pl.program_id(1) == 0)
    def _():
        st_ref[...] = jnp.zeros_like(st_ref)

    ri = lax.broadcasted_iota(I32, (C, HK), 0)
    li = lax.broadcasted_iota(I32, (C, HK), 1)
    head_of_lane = _div_pow2(li, DK)
    sr = _div_pow2(lax.broadcasted_iota(I32, (HV, HK), 0), DV)
    scol = _div_pow2(lax.broadcasted_iota(I32, (HV, HK), 1), DK)
    st_mask = sr == scol
    tril = tril_ref[...]
    expand = expand_ref[...]
    gain = gain_ref[...]

    def chunk(ci, carry):
        r0 = pl.multiple_of(ci * C, C)
        blk = qkvr_ref[0, pl.ds(r0, C), :]
        q = blk[:, 0:HK].astype(F32)
        k = blk[:, HK:2 * HK].astype(F32)
        v = blk[:, 2 * HK:2 * HK + HV].astype(F32)
        r = blk[:, 2 * HK + HV:].astype(F32)
        g = g_ref[0, pl.ds(r0, C), :]
        g1 = g.astype(BF16)
        rem = g - g1.astype(F32)
        g2 = rem.astype(BF16)
        g3 = (rem - g2.astype(F32)).astype(BF16)
        b = _dot(tril, g1) + _dot(tril, g2) + _dot(tril, g3)
        b_last = b[C - 1:C, :]
        st = st_ref[...]
        o = _dot_nt((q * jnp.exp(b)).astype(BF16), st.astype(BF16))
        p_acc = jnp.zeros((H * C, C), F32)
        for I in range(1, C // S):
            beta = b[I * S - 1:I * S, :]
            in_I = (ri >= I * S) & (ri < (I + 1) * S)
            qI = jnp.where(in_I, q * jnp.exp(jnp.minimum(b - beta, 0.0)), 0.0)
            kI = jnp.where(ri < I * S, k * jnp.exp(jnp.minimum(beta - b, 0.0)), 0.0)
            q_st = jnp.concatenate([jnp.where(head_of_lane == h, qI, 0.0) for h in range(H)], axis=0)
            p_acc = p_acc + _dot_nt(q_st.astype(BF16), kI.astype(BF16))
        p_b = p_acc.astype(BF16)
        vb = v.astype(BF16)
        o_off = jnp.concatenate(
            [_dot(p_b[h * C:(h + 1) * C, :], vb[:, h * DV:(h + 1) * DV]) for h in range(H)], axis=1)
        o = o + o_off
        GRP = 4
        for d0 in range(0, S, GRP):
            ts = []
            for d in range(d0, d0 + GRP):
                if d == 0:
                    ts.append(q * k)
                else:
                    kd = pltpu.roll(k, d, axis=0)
                    bd = pltpu.roll(b, d, axis=0)
                    valid = _mod_pow2(ri, S) >= d
                    ts.append(q * kd * jnp.exp(jnp.where(valid, jnp.minimum(b - bd, 0.0), -jnp.inf)))
            s_all = _dot(jnp.concatenate(ts, axis=0).astype(BF16), expand)
            for n, d in enumerate(range(d0, d0 + GRP)):
                vd = v if d == 0 else pltpu.roll(v, d, axis=0)
                o = o + s_all[n * C:(n + 1) * C, :] * vd
        k_end = k * jnp.exp(b_last - b)
        upd = _dot_tn(vb, k_end.astype(BF16))
        st_ref[...] = st * jnp.exp(b_last) + jnp.where(st_mask, upd, 0.0)
        outs = []
        for h in range(H):
            oh = o[:, h * DV:(h + 1) * DV]
            ms = jnp.mean(oh * oh, axis=-1, keepdims=True)
            outs.append(oh * lax.rsqrt(ms + RMS_EPS))
        on = jnp.concatenate(outs, axis=1) * gain
        y = on * (r * jax.nn.sigmoid(r))
        y_ref[0, pl.ds(r0, C), :] = y.astype(y_ref.dtype)
        return carry

    lax.fori_loop(0, rows // C, chunk, 0, unroll=2)


def _gla(qkvr, g, gain, rows):
    B, L, W = qkvr.shape
    HK, HV = GLA_HEADS * GLA_DK, GLA_HEADS * GLA_DV
    C = GLA_CHUNK
    tril = jnp.asarray(np.tril(np.ones((C, C), np.float32)), BF16)
    expand = jnp.asarray(np.kron(np.eye(GLA_HEADS, dtype=np.float32), np.ones((GLA_DK, GLA_DV), np.float32)), BF16)
    return pl.pallas_call(
        functools.partial(_gla_kernel, rows=rows),
        grid=(B, L // rows),
        in_specs=[pl.BlockSpec((1, rows, W), lambda b, i: (b, i, 0)),
                  pl.BlockSpec((1, rows, HK), lambda b, i: (b, i, 0)),
                  pl.BlockSpec((1, HV), lambda b, i: (0, 0)),
                  pl.BlockSpec((C, C), lambda b, i: (0, 0)),
                  pl.BlockSpec((HK, HV), lambda b, i: (0, 0))],
        out_specs=pl.BlockSpec((1, rows, HV), lambda b, i: (b, i, 0)),
        out_shape=jax.ShapeDtypeStruct((B, L, HV), BF16),
        scratch_shapes=[pltpu.VMEM((HV, HK), F32)],
        compiler_params=_cparams(("arbitrary", "arbitrary")),
        name="gla",
    )(qkvr, g, gain, tril, expand)


def _bit_transpose32(words):
    a = list(words)
    assert len(a) == WORD_BITS
    j, m = WORD_BITS // 2, 0x0000FFFF
    while j:
        m_i32 = m - (1 << 32) if m >= (1 << 31) else m
        for k in range(WORD_BITS):
            if k & j == 0:
                t = (a[k] ^ lax.shift_right_logical(a[k + j], jnp.int32(j))) & m_i32
                a[k] = a[k] ^ t
                a[k + j] = a[k + j] ^ lax.shift_left(t, jnp.int32(j))
        j >>= 1
        m = (m ^ (m << j)) & 0xFFFFFFFF
    return a


def _dsa_kernel(dq_ref, kv_ref, kvt_ref, iq_ref, ik_ref, iw_ref, o_ref,
                key_ref, plane_ref, cand_ref, qt_ref, qit_ref, wt_ref, m_ref, acc_ref, s_ref, smax_ref, p_ref,
                *, tq, tk, top_k):
    NH, DH = DSA_HEADS, DSA_HEAD_DIM
    assert tq == LANES
    i = pl.program_id(1)
    t0 = i * tq
    nkb = (t0 + tq + tk - 1) // tk

    qt_ref[DH:2 * DH, :] = jnp.zeros((DH, NH * tq), BF16)
    for p in range(NH // 2):
        pair_t = dq_ref[0, :, p * LANES:(p + 1) * LANES].astype(F32).T
        qt_ref[0:DH, (2 * p) * tq:(2 * p + 1) * tq] = pair_t[0:DH].astype(BF16)
        qt_ref[0:DH, (2 * p + 1) * tq:(2 * p + 2) * tq] = pair_t[DH:].astype(BF16)
    qit_ref[...] = jnp.zeros(qit_ref.shape, BF16)
    heads_per_group = LANES // IDX_DIM
    for g in range(IDX_HEADS // heads_per_group):
        grp_t = iq_ref[0, :, g * LANES:(g + 1) * LANES].astype(F32).T
        for u in range(heads_per_group):
            h = g * heads_per_group + u
            qit_ref[0:IDX_DIM, h * tq:(h + 1) * tq] = grp_t[u * IDX_DIM:(u + 1) * IDX_DIM].astype(BF16)
    wt_ref[...] = iw_ref[0].T[0:SUBLANES]

    q_pos = t0 + lax.broadcasted_iota(I32, (tk, tq), 1)
    k_off = lax.broadcasted_iota(I32, (tk, tq), 0)
    q_pos8 = t0 + lax.broadcasted_iota(I32, (SUBLANES, tq), 1)
    sub8 = lax.broadcasted_iota(I32, (SUBLANES, tq), 0)
    groups_per_block = tk // (WORD_BITS * SUBLANES)

    @pl.when((pl.program_id(0) == 0) & (i == 0))
    def _():
        plane_ref[...] = jnp.zeros(plane_ref.shape, I32)
        eye = (lax.broadcasted_iota(I32, (tq, tq), 0) == lax.broadcasted_iota(I32, (tq, tq), 1)).astype(F32)
        for h in range(NH):
            qt_ref[2 * DH:, h * tq:(h + 1) * tq] = eye.astype(BF16)

    def score_keys(j):
        k0 = pl.multiple_of(j * tk, tk)
        ikb = ik_ref[0, pl.ds(k0, tk), :]
        sc = jnp.zeros((tk, tq), F32)
        for c in range(IDX_HEADS // 2):
            lg = _dot(ikb, qit_ref[:, (2 * c) * tq:(2 * c + 2) * tq])
            for u in range(2):
                h = 2 * c + u
                sc = sc + jnp.maximum(lg[:, u * tq:(u + 1) * tq], 0.0) * wt_ref[h:h + 1, :]
        sc = sc + 0.0
        bits = pltpu.bitcast(sc, I32)
        key = bits ^ ((bits >> 31) & 0x7FFFFFFF)
        key_ref[j] = jnp.where(k0 + k_off <= q_pos, key, INT_MIN)

    def bit_planes(j):
        k0 = j * tk
        ukey = key_ref[j] ^ INT_MIN
        for g in range(groups_per_block):
            words = [ukey[(g * WORD_BITS + w) * SUBLANES:(g * WORD_BITS + w + 1) * SUBLANES, :]
                     for w in range(WORD_BITS)]
            planes = _bit_transpose32(words)
            r0 = pl.multiple_of((j * groups_per_block + g) * SUBLANES, SUBLANES)
            for p in range(WORD_BITS):
                plane_ref[p, pl.ds(r0, SUBLANES), :] = planes[p]
            n_adm = ((q_pos8 - (k0 + g * WORD_BITS * SUBLANES) - sub8) >> _log2(SUBLANES)) + 1
            n_adm = jnp.clip(n_adm, 0, WORD_BITS)
            cand_ref[pl.ds(r0, SUBLANES), :] = jnp.where(
                n_adm == 0, 0, lax.shift_left(jnp.int32(-1), WORD_BITS - jnp.maximum(n_adm, 1)))

    cand_ref[...] = jnp.zeros(cand_ref.shape, I32)
    last = nkb - 1
    score_keys(0)

    def score_step(j, carry):
        bit_planes(j)
        score_keys(jnp.minimum(j + 1, last))
        return carry

    lax.fori_loop(0, nkb, score_step, 0)

    def popcount_rows(words):
        pc = lax.population_count(words)
        parts = [pc[r * SUBLANES:(r + 1) * SUBLANES, :] for r in range(words.shape[0] // SUBLANES)]
        while len(parts) > 1:
            parts = [a + b for a, b in zip(parts[0::2], parts[1::2])]
        return jnp.sum(parts[0].astype(F32), axis=0, keepdims=True)

    def select_bit(p, carry):
        thr_u, rank = carry
        plane = plane_ref[p]
        cand = cand_ref[...]
        ones = cand & plane
        n_ones = popcount_rows(ones)
        take = n_ones >= rank
        thr_u = thr_u | jnp.where(take, lax.shift_left(jnp.int32(1), WORD_BITS - 1 - p), 0)
        rank = jnp.where(take, rank, rank - n_ones)
        cand_ref[...] = cand & (plane ^ jnp.where(take, 0, -1))
        return thr_u, rank

    thr_u, rank = lax.fori_loop(0, WORD_BITS, select_bit,
                                (jnp.zeros((1, tq), I32), jnp.full((1, tq), top_k, F32)))
    thr = thr_u ^ INT_MIN
    n_tied = popcount_rows(cand_ref[...])

    def count_rows(hit_fn):
        acc_rows = 8 * SUBLANES

        def body(j, cnt):
            hit = hit_fn(j)
            return cnt + jnp.sum(hit.reshape(tk // acc_rows, acc_rows, tq), axis=0)

        cnt = lax.fori_loop(0, nkb, body, jnp.zeros((acc_rows, tq), F32))
        return jnp.sum(cnt, axis=0, keepdims=True)

    thr = jnp.maximum(thr, INT_MIN + 1)
    excess = jnp.max(jnp.where(n_tied > rank, 1.0, 0.0))

    @pl.when(excess > 0.0)
    def _():
        need = rank
        nbits = max(1, int(math.ceil(math.log2(key_ref.shape[0] * tk))))

        def count_eq_le(idx_lim):
            return count_rows(
                lambda j: jnp.where((key_ref[j] == thr) & (j * tk + k_off <= idx_lim), 1.0, 0.0))

        def idx_bit(it, lo):
            trial = lo + lax.shift_left(jnp.int32(1), nbits - 1 - it)
            return jnp.where(count_eq_le(trial - 1) < need, trial, lo)

        cut = lax.fori_loop(0, nbits, idx_bit, jnp.zeros((1, tq), I32))

        def drop(j, carry):
            key = key_ref[j]
            key_ref[j] = jnp.where((key == thr) & (j * tk + k_off > cut), INT_MIN, key)
            return carry

        lax.fori_loop(0, nkb, drop, 0)

    m_ref[...] = jnp.full(m_ref.shape, NEG_BIG, F32)
    acc_ref[...] = jnp.zeros(acc_ref.shape, F32)
    cw = 2 * tq
    rc = 4 * SUBLANES
    ones_rows = jnp.ones((acc_ref.shape[0] - DH, tk), BF16)

    def logits(j, slot, live):
        k0 = pl.multiple_of(j * tk, tk)
        kvb = kv_ref[0, pl.ds(k0, tk), :]
        floor = jnp.where(live, jnp.int32(INT_MIN), jnp.int32(INT_MAX))
        bias = jnp.where(key_ref[j] >= jnp.maximum(thr, floor), 0.0, NEG_BIG)
        lhs = jnp.concatenate([kvb, bias.astype(BF16)], axis=1)
        for c in range(NH // 2):
            cols = slice(c * cw, (c + 1) * cw)
            s = _dot(lhs, qt_ref[:, cols])
            s_ref[slot, :, cols] = s
            smax_ref[slot, :, cols] = jnp.max(s, axis=0, keepdims=True)

    def softmax_values(j, slot):
        m_old = m_ref[...]
        m_new = jnp.maximum(m_old, smax_ref[slot])
        alpha = jnp.exp2(m_old - m_new)
        m_ref[...] = m_new
        vt1 = jnp.concatenate([kvt_ref[0, j, DH:, :], ones_rows], axis=0)
        for c in range(NH // 2):
            cols = slice(c * cw, (c + 1) * cw)
            m_rows = jnp.broadcast_to(m_new[:, cols], (rc, cw))
            for r in range(tk // rc):
                p = jnp.exp2(s_ref[slot, r * rc:(r + 1) * rc, cols] - m_rows)
                p_ref[r * rc:(r + 1) * rc, cols] = p.astype(BF16)
            acc_ref[:, cols] = acc_ref[:, cols] * alpha[:, cols] + _dot(vt1, p_ref[:, cols])

    logits(0, 0, True)

    def attend_pair(jj, carry):
        j0 = 2 * jj
        logits(jnp.minimum(j0 + 1, last), 1, j0 + 1 <= last)
        softmax_values(j0, 0)
        logits(jnp.minimum(j0 + 2, last), 0, j0 + 2 <= last)
        softmax_values(jnp.minimum(j0 + 1, last), 1)
        return carry

    lax.fori_loop(0, (nkb + 1) // 2, attend_pair, 0)

    for p in range(NH // 2):
        pair_t = jnp.concatenate(
            [acc_ref[0:DH, (2 * p + u) * tq:(2 * p + u + 1) * tq]
             / acc_ref[DH:DH + 1, (2 * p + u) * tq:(2 * p + u + 1) * tq]
             for u in range(2)], axis=0)
        o_ref[0, :, p * LANES:(p + 1) * LANES] = pair_t.T.astype(o_ref.dtype)


def _dsa(dq, dkv, dkvt, iq, ik, iw, tq, tk):
    B, L, _ = dq.shape
    top_k = min(DSA_TOPK_MAX, L // 4)
    nkb_max = L // tk
    assert dkvt.shape == (B, nkb_max, dkv.shape[2], tk)
    return pl.pallas_call(
        functools.partial(_dsa_kernel, tq=tq, tk=tk, top_k=float(top_k)),
        grid=(B, L // tq),
        in_specs=[pl.BlockSpec((1, tq, dq.shape[2]), lambda b, i: (b, i, 0)),
                  pl.BlockSpec((1, L, dkv.shape[2]), lambda b, i: (b, 0, 0)),
                  pl.BlockSpec((1,) + dkvt.shape[1:], lambda b, i: (b, 0, 0, 0)),
                  pl.BlockSpec((1, tq, iq.shape[2]), lambda b, i: (b, i, 0)),
                  pl.BlockSpec((1, L, ik.shape[2]), lambda b, i: (b, 0, 0)),
                  pl.BlockSpec((1, tq, iw.shape[2]), lambda b, i: (b, i, 0))],
        out_specs=pl.BlockSpec((1, tq, dq.shape[2]), lambda b, i: (b, i, 0)),
        out_shape=jax.ShapeDtypeStruct(dq.shape, BF16),
        scratch_shapes=[pltpu.VMEM((nkb_max, tk, tq), I32),
                        pltpu.VMEM((WORD_BITS, L // WORD_BITS, tq), I32),
                        pltpu.VMEM((L // WORD_BITS, tq), I32),
                        pltpu.VMEM((2 * DSA_HEAD_DIM + tq, DSA_HEADS * tq), BF16),
                        pltpu.VMEM((LANES, IDX_HEADS * tq), BF16),
                        pltpu.VMEM((SUBLANES, tq), F32),
                        pltpu.VMEM((1, DSA_HEADS * tq), F32),
                        pltpu.VMEM((DSA_HEAD_DIM + 2 * SUBLANES, DSA_HEADS * tq), F32),
                        pltpu.VMEM((2, tk, DSA_HEADS * tq), F32),
                        pltpu.VMEM((2, 1, DSA_HEADS * tq), F32),
                        pltpu.VMEM((tk, DSA_HEADS * tq), BF16)],
        compiler_params=_cparams(("arbitrary", "arbitrary")),
        name="dsa",
    )(dq, dkv, dkvt, iq, ik, iw)


def _layer_norm(z, gain, bias):
    mu = jnp.mean(z, axis=-1, keepdims=True)
    zc = z - mu
    var = jnp.mean(zc * zc, axis=-1, keepdims=True)
    return zc * lax.rsqrt(var + LN_EPS) * gain + bias


def _merge_kernel(x_ref, yg_ref, yd_ref, wgate_ref, wbr_ref, wo_ref, gain_ref, bias_ref, o_ref, *, alpha):
    x = x_ref[0]
    D = x.shape[1]
    gates = jax.nn.sigmoid(_dot(x.astype(BF16), wgate_ref[...]))
    pg = _dot(yg_ref[0], wbr_ref[0])
    pd = _dot(yd_ref[0], wbr_ref[1])
    m = gates[:, 0:D] * pg + gates[:, D:] * pd
    mixed = _dot(m.astype(BF16), wo_ref[...])
    o_ref[0] = _layer_norm(alpha * x + mixed, gain_ref[...], bias_ref[...])


def _merge(x, yg, yd, wgate, wbr, wo, gain, bias, rows, alpha):
    B, L, D = x.shape

    def full(a):
        return pl.BlockSpec(a.shape, lambda b, i, _n=a.ndim: (0,) * _n)

    def rowblk(w):
        return pl.BlockSpec((1, rows, w), lambda b, i: (b, i, 0))

    return pl.pallas_call(
        functools.partial(_merge_kernel, alpha=alpha),
        grid=(B, L // rows),
        in_specs=[rowblk(D), rowblk(yg.shape[2]), rowblk(yd.shape[2]),
                  full(wgate), full(wbr), full(wo), full(gain), full(bias)],
        out_specs=rowblk(D),
        out_shape=jax.ShapeDtypeStruct((B, L, D), F32),
        compiler_params=_cparams(("arbitrary", "arbitrary")),
        name="merge_ln1",
    )(x, yg, yd, wgate, wbr, wo, gain, bias)


def _ffn_kernel(x_ref, wup_ref, cw_ref, cb_ref, wdn_ref, gain_ref, bias_ref, o_ref, hist_ref, u_ref, act_ref,
                *, alpha, n_chunks, cw, rsub):
    rows = x_ref.shape[1]
    H = SUBLANES

    @pl.when(pl.program_id(1) == 0)
    def _():
        hist_ref[...] = jnp.zeros_like(hist_ref)

    xb = x_ref[0].astype(BF16)

    def up_proj(c, slot):
        u_ref[slot, 0:H, :] = hist_ref[c]
        u_ref[slot, H:, :] = _dot(xb, wup_ref[c])
        hist_ref[c] = u_ref[slot, rows:rows + H, :]

    def conv_gate(c, slot):
        w = cw_ref[c]
        b = cb_ref[c]
        for r in range(rows // rsub):
            base = H + r * rsub
            cu = (b + u_ref[slot, base - 2:base - 2 + rsub, :] * w[0:1, :]
                  + u_ref[slot, base - 1:base - 1 + rsub, :] * w[1:2, :]
                  + u_ref[slot, base:base + rsub, :] * w[2:3, :])
            ug = cu[:, 0:cw]
            act = (ug * jax.nn.sigmoid(ug)) * cu[:, cw:]
            act_ref[c, r * rsub:(r + 1) * rsub, :] = act.astype(BF16)

    up_proj(0, 0)
    for c in range(1, n_chunks):
        up_proj(c, c % 2)
        conv_gate(c - 1, (c - 1) % 2)
    conv_gate(n_chunks - 1, (n_chunks - 1) % 2)
    act = jnp.concatenate([act_ref[c] for c in range(n_chunks)], axis=1)
    f = _dot(act, wdn_ref[...])
    o_ref[0] = _layer_norm(alpha * x_ref[0] + f, gain_ref[...], bias_ref[...])


def _ffn(x, wup, cwt, cbs, wdn, gain, bias, rows, alpha):
    B, L, D = x.shape
    n_chunks, _, cw2 = wup.shape

    def full(a):
        return pl.BlockSpec(a.shape, lambda b, i, _n=a.ndim: (0,) * _n, pipeline_mode=pl.Buffered(1))

    rowblk = pl.BlockSpec((1, rows, D), lambda b, i: (b, i, 0))
    return pl.pallas_call(
        functools.partial(_ffn_kernel, alpha=alpha, n_chunks=n_chunks, cw=cw2 // 2, rsub=min(rows, 64)),
        grid=(B, L // rows),
        in_specs=[rowblk, full(wup), full(cwt), full(cbs), full(wdn), full(gain), full(bias)],
        out_specs=rowblk,
        out_shape=jax.ShapeDtypeStruct((B, L, D), F32),
        scratch_shapes=[pltpu.VMEM((n_chunks, SUBLANES, cw2), F32),
                        pltpu.VMEM((2, rows + SUBLANES, cw2), F32),
                        pltpu.VMEM((n_chunks, rows, cw2 // 2), BF16)],
        compiler_params=_cparams(("arbitrary", "arbitrary")),
        name="ffn_ln2",
    )(x, wup, cwt, cbs, wdn, gain, bias)


def _pad_cols(w, width):
    return jnp.pad(w, ((0, 0), (0, width - w.shape[1])))


def _ffn_chunk_width(d_ff):
    for cw in (512, 384, 256, 128):
        if d_ff % cw == 0:
            return cw
    raise ValueError(f"d_ff={d_ff} must be a multiple of {LANES}")


def _token_mixers(x, w_in, w_a2, b_a, gla_gain):
    B, L, D = x.shape
    nqk = GLA_HEADS * GLA_DK
    nv = GLA_HEADS * GLA_DV
    ndq = DSA_HEADS * DSA_HEAD_DIM
    niq = IDX_HEADS * IDX_DIM
    widths = (nqk, nqk, nv, GLA_GATE_RANK, nv, ndq, DSA_HEAD_DIM, DSA_HEAD_DIM, niq, IDX_DIM, IDX_HEADS, N_BRANCH * D)
    offs = np.concatenate([[0], np.cumsum(widths)])
    assert w_in.shape == (D, int(offs[-1]))
    col = lambda n: w_in[:, int(offs[n]):int(offs[n + 1])]
    w_gq, w_gk, w_gv, w_ga, w_gr, w_dq, w_dk, w_dv, w_iq, w_ik, w_iw, w_gate = [col(n) for n in range(12)]
    wts = (
        jnp.concatenate([w_gq, w_gk, w_gv, w_gr], axis=1).astype(BF16),
        _pad_cols(w_ga, LANES).astype(BF16),
        jnp.pad(w_a2, ((0, LANES - GLA_GATE_RANK), (0, 0))).astype(BF16),
        b_a.reshape(1, nqk).astype(F32),
        w_dq.astype(BF16),
        jnp.concatenate([w_dk, w_dv], axis=1).astype(BF16),
        w_iq.astype(BF16),
        _pad_cols(w_ik, LANES).astype(BF16),
        _pad_cols(w_iw, LANES).astype(BF16),
    )
    def lane_freqs(head_dim):
        rot = head_dim // ROPE_FRACTION
        half = rot // 2
        inv = ROPE_THETA ** (-jnp.arange(half, dtype=F32) * 2.0 / rot)
        m = np.arange(LANES) % head_dim
        return jnp.where(jnp.asarray(m < rot), inv[jnp.asarray(m % half)], 0.0)

    invf = jnp.stack([lane_freqs(DSA_HEAD_DIM), lane_freqs(IDX_DIM)]).reshape(2, 1, LANES)
    tabs = _rope_tables(L, invf)

    tq = LANES
    tk = min(L, 512)
    gla_in, g, dq, dkv, dkvt, iq, ik, iw = _in_proj(x, wts, tabs, rows=tk)
    y_gla = _gla(gla_in, g, gla_gain.reshape(1, nv).astype(F32), rows=min(L, 512))
    y_dsa = _dsa(dq, dkv, dkvt, iq, ik, iw, tq, tk)
    return y_gla, y_dsa, w_gate


def _layer(x, w_in, w_a2, b_a, gla_gain, w_branch, w_o, ln1_g, ln1_b, w_up, conv_w, conv_b, w_down, ln2_g, ln2_b,
           depth):
    B, L, D = x.shape
    alpha = (2.0 * depth) ** 0.25
    y_gla, y_dsa, w_gate = _token_mixers(x, w_in, w_a2, b_a, gla_gain)
    x1 = _merge(x, y_gla, y_dsa, w_gate.astype(BF16), w_branch.astype(BF16), w_o.astype(BF16),
                ln1_g.reshape(1, D), ln1_b.reshape(1, D), rows=min(L, 512), alpha=alpha)

    d_ff = w_down.shape[0]
    cw = _ffn_chunk_width(d_ff)
    nch = d_ff // cw
    chunked = lambda a: jnp.concatenate(
        [a[..., :d_ff].reshape(a.shape[:-1] + (nch, cw)), a[..., d_ff:].reshape(a.shape[:-1] + (nch, cw))], axis=-1)
    wup = jnp.moveaxis(chunked(w_up), 1, 0).astype(BF16)
    cwt = jnp.moveaxis(chunked(conv_w), 1, 0)
    cwt = jnp.pad(cwt, ((0, 0), (0, SUBLANES - CONV_WIDTH), (0, 0)))
    cbs = jnp.moveaxis(chunked(conv_b[None, :]), 1, 0)
    return _ffn(x1, wup, cwt, cbs, w_down.astype(BF16), ln2_g.reshape(1, D), ln2_b.reshape(1, D),
                rows=min(L, 512), alpha=alpha)


def kernel(x, w_in, w_gla_a2, b_gla_a, gla_norm_gain, w_branch, w_o, ln1_gain, ln1_bias,
           w_up, conv_w, conv_b, w_down, ln2_gain, ln2_bias):
    depth = w_in.shape[0]
    for layer in range(depth):
        x = _layer(x, w_in[layer], w_gla_a2[layer], b_gla_a[layer], gla_norm_gain[layer], w_branch[layer],
                   w_o[layer], ln1_gain[layer], ln1_bias[layer], w_up[layer], conv_w[layer], conv_b[layer],
                   w_down[layer], ln2_gain[layer], ln2_bias[layer], depth)
    return x
```

```python
import functools
import math

import numpy as np
import jax
import jax.numpy as jnp
from jax import lax
from jax.experimental import pallas as pl
from jax.experimental.pallas import tpu as pltpu

F32 = jnp.float32
BF16 = jnp.bfloat16
I32 = jnp.int32

GLA_HEADS = 4
GLA_DK = 64
GLA_DV = 128
GLA_GATE_RANK = 16
GLA_TAU = 16.0
DSA_HEADS = 8
DSA_HEAD_DIM = 64
IDX_HEADS = 8
IDX_DIM = 32
DSA_TOPK_MAX = 256
ROPE_THETA = 500000.0
ROPE_FRACTION = 4
N_BRANCH = 2
BRANCH_WIDTH = 512
CONV_WIDTH = 3
LN_EPS = 1e-5
RMS_EPS = 1e-6

LANES = 128
SUBLANES = 8
VMEM_LIMIT_BYTES = 56 * 1024 * 1024

GLA_CHUNK = 64
GLA_SUB = 16
WORD_BITS = 32
INT_MIN = -(2 ** 31)
INT_MAX = 2 ** 31 - 1
NEG_BIG = -1e30
LOG2_E = 1.4426950408889634


def _cparams(sem):
    return pltpu.CompilerParams(dimension_semantics=sem, vmem_limit_bytes=VMEM_LIMIT_BYTES)


def _log2(n):
    assert n > 0 and n & (n - 1) == 0, f"{n} must be a power of two"
    return n.bit_length() - 1


def _div_pow2(x, n):
    return lax.shift_right_logical(x, jnp.int32(_log2(n)))


def _mod_pow2(x, n):
    _log2(n)
    return x & (n - 1)


def _dot(a, b):
    return jnp.dot(a, b, preferred_element_type=F32)


def _dot_nt(a, b):
    return lax.dot_general(a, b, (((1,), (1,)), ((), ())), preferred_element_type=F32)


def _dot_tn(a, b):
    return lax.dot_general(a, b, (((0,), (0,)), ((), ())), preferred_element_type=F32)


def _rope_table_kernel(invf_ref, c_ref, sa_ref, sb_ref, *, rows, periods, halves):
    t0 = pl.program_id(0) * rows
    pos = (t0 + lax.broadcasted_iota(I32, (rows, LANES), 0)).astype(F32)
    lane = lax.broadcasted_iota(I32, (rows, LANES), 1)
    for p, (period, half) in enumerate(zip(periods, halves)):
        ang = pos * invf_ref[p]
        c = jnp.cos(ang)
        s = jnp.sin(ang)
        m = _mod_pow2(lane, period)
        is1 = m < half
        is2 = (m >= half) & (m < 2 * half)
        c_ref[p] = jnp.where(is1 | is2, c, 1.0)
        sa_ref[p] = jnp.where(is1, -s, 0.0)
        sb_ref[p] = jnp.where(is2, s, 0.0)


def _rope_tables(L, invf):
    rows = min(L, 512)
    n = invf.shape[0]
    spec = pl.BlockSpec((n, rows, LANES), lambda i: (0, i, 0))
    shp = jax.ShapeDtypeStruct((n, L, LANES), F32)
    return pl.pallas_call(
        functools.partial(_rope_table_kernel, rows=rows, periods=(DSA_HEAD_DIM, IDX_DIM),
                          halves=(DSA_HEAD_DIM // ROPE_FRACTION // 2, IDX_DIM // ROPE_FRACTION // 2)),
        grid=(L // rows,),
        in_specs=[pl.BlockSpec((n, 1, LANES), lambda i: (0, 0, 0))],
        out_specs=[spec, spec, spec],
        out_shape=[shp, shp, shp],
        compiler_params=_cparams(("arbitrary",)),
        name="rope_tables",
    )(invf)


def _apply_rope(x, c, sa, sb, half, lane_limit=None):
    outs = []
    for g in range(x.shape[1] // LANES):
        xs = x[:, g * LANES:(g + 1) * LANES]
        up = pltpu.roll(xs, LANES - half, axis=1)
        dn = pltpu.roll(xs, half, axis=1)
        y = xs * c + up * sa + dn * sb
        if lane_limit is not None:
            lane = lax.broadcasted_iota(I32, xs.shape, 1)
            y = jnp.where(lane < lane_limit, y, xs)
        outs.append(y)
    return outs[0] if len(outs) == 1 else jnp.concatenate(outs, axis=1)


def _in_proj_kernel(x_ref, wgla_ref, wga_ref, wa2_ref, ba_ref, wdq_ref, wdkv_ref, wiq_ref, wik_ref, wiw_ref,
                    c_ref, sa_ref, sb_ref,
                    gla_ref, g_ref, dq_ref, dkv_ref, dkvt_ref, iq_ref, ik_ref, iw_ref):
    xb = x_ref[0].astype(BF16)
    hg = _dot(xb, wgla_ref[...])
    nqk = GLA_HEADS * GLA_DK
    gla_ref[0, :, 0:nqk] = (hg[:, 0:nqk] * (GLA_DK ** -0.5)).astype(BF16)
    gla_ref[0, :, nqk:] = hg[:, nqk:].astype(BF16)
    a_low = _dot(xb, wga_ref[...])
    z = _dot(a_low.astype(BF16), wa2_ref[...]) + ba_ref[...]
    g_ref[0] = (jnp.minimum(z, 0.0) - jnp.log1p(jnp.exp(-jnp.abs(z)))) * (1.0 / GLA_TAU)
    c64, sa64, sb64 = c_ref[0], sa_ref[0], sb_ref[0]
    c32, sa32, sb32 = c_ref[1], sa_ref[1], sb_ref[1]
    h64 = DSA_HEAD_DIM // ROPE_FRACTION // 2
    h32 = IDX_DIM // ROPE_FRACTION // 2
    dq = _dot(xb, wdq_ref[...])
    dq_ref[0] = (_apply_rope(dq, c64, sa64, sb64, h64) * (DSA_HEAD_DIM ** -0.5 * LOG2_E)).astype(BF16)
    dkv = _apply_rope(_dot(xb, wdkv_ref[...]), c64, sa64, sb64, h64, lane_limit=DSA_HEAD_DIM)
    dkv_ref[0] = dkv.astype(BF16)
    dkvt_ref[0, 0] = dkv.T.astype(BF16)
    iq = _dot(xb, wiq_ref[...])
    iq_ref[0] = _apply_rope(iq, c32, sa32, sb32, h32).astype(BF16)
    ik = _dot(xb, wik_ref[...])
    ik_ref[0] = _apply_rope(ik, c32, sa32, sb32, h32).astype(BF16)
    iw_ref[0] = _dot(xb, wiw_ref[...]) * ((IDX_HEADS ** -0.5) * (IDX_DIM ** -0.5))


def _in_proj(x, wts, tabs, rows):
    B, L, D = x.shape
    wgla, wga, wa2, ba, wdq, wdkv, wiq, wik, wiw = wts
    c, sa, sb = tabs
    n_l = L // rows

    def full(a):
        return pl.BlockSpec(a.shape, lambda i, b, _n=a.ndim: (0,) * _n)

    def rowblk(w):
        return pl.BlockSpec((1, rows, w), lambda i, b: (b, i, 0))

    tab = pl.BlockSpec((2, rows, LANES), lambda i, b: (0, i, 0))
    widths = (wgla.shape[1], GLA_HEADS * GLA_DK, wdq.shape[1], wdkv.shape[1], wiq.shape[1], wik.shape[1], wiw.shape[1])
    dtypes = (BF16, F32, BF16, BF16, BF16, BF16, F32)
    out_specs = [rowblk(w) for w in widths]
    out_shape = [jax.ShapeDtypeStruct((B, L, w), dt) for w, dt in zip(widths, dtypes)]
    out_specs.insert(4, pl.BlockSpec((1, 1, wdkv.shape[1], rows), lambda i, b: (b, i, 0, 0)))
    out_shape.insert(4, jax.ShapeDtypeStruct((B, n_l, wdkv.shape[1], rows), BF16))
    return pl.pallas_call(
        _in_proj_kernel,
        grid=(n_l, B),
        in_specs=[rowblk(D)] + [full(w) for w in wts] + [tab, tab, tab],
        out_specs=out_specs,
        out_shape=out_shape,
        compiler_params=_cparams(("arbitrary", "arbitrary")),
        name="in_proj",
    )(x, *wts, c, sa, sb)


def _gla_kernel(qkvr_ref, g_ref, gain_ref, tril_ref, expand_ref, y_ref, st_ref, *, rows):
    C, S = GLA_CHUNK, GLA_SUB
    H, DK, DV = GLA_HEADS, GLA_DK, GLA_DV
    HK, HV = H * DK, H * DV

    @pl.when(pl.program_id(1) == 0)
    def _():
        st_ref[...] = jnp.zeros_like(st_ref)

    ri = lax.broadcasted_iota(I32, (C, HK), 0)
    li = lax.broadcasted_iota(I32, (C, HK), 1)
    head_of_lane = _div_pow2(li, DK)
    sr = _div_pow2(lax.broadcasted_iota(I32, (HV, HK), 0), DV)
    scol = _div_pow2(lax.broadcasted_iota(I32, (HV, HK), 1), DK)
    st_mask = sr == scol
    tril = tril_ref[...]
    expand = expand_ref[...]
    gain = gain_ref[...]

    def chunk(ci, carry):
        r0 = pl.multiple_of(ci * C, C)
        blk = qkvr_ref[0, pl.ds(r0, C), :]
        q = blk[:, 0:HK].astype(F32)
        k = blk[:, HK:2 * HK].astype(F32)
        v = blk[:, 2 * HK:2 * HK + HV].astype(F32)
        r = blk[:, 2 * HK + HV:].astype(F32)
        g = g_ref[0, pl.ds(r0, C), :]
        g1 = g.astype(BF16)
        rem = g - g1.astype(F32)
        g2 = rem.astype(BF16)
        g3 = (rem - g2.astype(F32)).astype(BF16)
        b = _dot(tril, g1) + _dot(tril, g2) + _dot(tril, g3)
        b_last = b[C - 1:C, :]
        st = st_ref[...]
        o = _dot_nt((q * jnp.exp(b)).astype(BF16), st.astype(BF16))
        p_acc = jnp.zeros((H * C, C), F32)
        for I in range(1, C // S):
            beta = b[I * S - 1:I * S, :]
            in_I = (ri >= I * S) & (ri < (I + 1) * S)
            qI = jnp.where(in_I, q * jnp.exp(jnp.minimum(b - beta, 0.0)), 0.0)
            kI = jnp.where(ri < I * S, k * jnp.exp(jnp.minimum(beta - b, 0.0)), 0.0)
            q_st = jnp.concatenate([jnp.where(head_of_lane == h, qI, 0.0) for h in range(H)], axis=0)
            p_acc = p_acc + _dot_nt(q_st.astype(BF16), kI.astype(BF16))
        p_b = p_acc.astype(BF16)
        vb = v.astype(BF16)
        o_off = jnp.concatenate(
            [_dot(p_b[h * C:(h + 1) * C, :], vb[:, h * DV:(h + 1) * DV]) for h in range(H)], axis=1)
        o = o + o_off
        GRP = 4
        for d0 in range(0, S, GRP):
            ts = []
            for d in range(d0, d0 + GRP):
                if d == 0:
                    ts.append(q * k)
                else:
                    kd = pltpu.roll(k, d, axis=0)
                    bd = pltpu.roll(b, d, axis=0)
                    valid = _mod_pow2(ri, S) >= d
                    ts.append(q * kd * jnp.exp(jnp.where(valid, jnp.minimum(b - bd, 0.0), -jnp.inf)))
            s_all = _dot(jnp.concatenate(ts, axis=0).astype(BF16), expand)
            for n, d in enumerate(range(d0, d0 + GRP)):
                vd = v if d == 0 else pltpu.roll(v, d, axis=0)
                o = o + s_all[n * C:(n + 1) * C, :] * vd
        k_end = k * jnp.exp(b_last - b)
        upd = _dot_tn(vb, k_end.astype(BF16))
        st_ref[...] = st * jnp.exp(b_last) + jnp.where(st_mask, upd, 0.0)
        outs = []
        for h in range(H):
            oh = o[:, h * DV:(h + 1) * DV]
            ms = jnp.mean(oh * oh, axis=-1, keepdims=True)
            outs.append(oh * lax.rsqrt(ms + RMS_EPS))
        on = jnp.concatenate(outs, axis=1) * gain
        y = on * (r * jax.nn.sigmoid(r))
        y_ref[0, pl.ds(r0, C), :] = y.astype(y_ref.dtype)
        return carry

    lax.fori_loop(0, rows // C, chunk, 0, unroll=2)


def _gla(qkvr, g, gain, rows):
    B, L, W = qkvr.shape
    HK, HV = GLA_HEADS * GLA_DK, GLA_HEADS * GLA_DV
    C = GLA_CHUNK
    tril = jnp.asarray(np.tril(np.ones((C, C), np.float32)), BF16)
    expand = jnp.asarray(np.kron(np.eye(GLA_HEADS, dtype=np.float32), np.ones((GLA_DK, GLA_DV), np.float32)), BF16)
    return pl.pallas_call(
        functools.partial(_gla_kernel, rows=rows),
        grid=(B, L // rows),
        in_specs=[pl.BlockSpec((1, rows, W), lambda b, i: (b, i, 0)),
                  pl.BlockSpec((1, rows, HK), lambda b, i: (b, i, 0)),
                  pl.BlockSpec((1, HV), lambda b, i: (0, 0)),
                  pl.BlockSpec((C, C), lambda b, i: (0, 0)),
                  pl.BlockSpec((HK, HV), lambda b, i: (0, 0))],
        out_specs=pl.BlockSpec((1, rows, HV), lambda b, i: (b, i, 0)),
        out_shape=jax.ShapeDtypeStruct((B, L, HV), BF16),
        scratch_shapes=[pltpu.VMEM((HV, HK), F32)],
        compiler_params=_cparams(("arbitrary", "arbitrary")),
        name="gla",
    )(qkvr, g, gain, tril, expand)


def _bit_transpose32(words):
    a = list(words)
    assert len(a) == WORD_BITS
    j, m = WORD_BITS // 2, 0x0000FFFF
    while j:
        m_i32 = m - (1 << 32) if m >= (1 << 31) else m
        for k in range(WORD_BITS):
            if k & j == 0:
                t = (a[k] ^ lax.shift_right_logical(a[k + j], jnp.int32(j))) & m_i32
                a[k] = a[k] ^ t
                a[k + j] = a[k + j] ^ lax.shift_left(t, jnp.int32(j))
        j >>= 1
        m = (m ^ (m << j)) & 0xFFFFFFFF
    return a


def _dsa_kernel(dq_ref, kv_ref, kvt_ref, iq_ref, ik_ref, iw_ref, o_ref,
                key_ref, plane_ref, cand_ref, qt_ref, qit_ref, wt_ref, m_ref, acc_ref, s_ref, smax_ref, p_ref,
                *, tq, tk, top_k):
    NH, DH = DSA_HEADS, DSA_HEAD_DIM
    assert tq == LANES
    i = pl.program_id(1)
    t0 = i * tq
    nkb = (t0 + tq + tk - 1) // tk

    qt_ref[DH:2 * DH, :] = jnp.zeros((DH, NH * tq), BF16)
    for p in range(NH // 2):
        pair_t = dq_ref[0, :, p * LANES:(p + 1) * LANES].astype(F32).T
        qt_ref[0:DH, (2 * p) * tq:(2 * p + 1) * tq] = pair_t[0:DH].astype(BF16)
        qt_ref[0:DH, (2 * p + 1) * tq:(2 * p + 2) * tq] = pair_t[DH:].astype(BF16)
    qit_ref[...] = jnp.zeros(qit_ref.shape, BF16)
    heads_per_group = LANES // IDX_DIM
    for g in range(IDX_HEADS // heads_per_group):
        grp_t = iq_ref[0, :, g * LANES:(g + 1) * LANES].astype(F32).T
        for u in range(heads_per_group):
            h = g * heads_per_group + u
            qit_ref[0:IDX_DIM, h * tq:(h + 1) * tq] = grp_t[u * IDX_DIM:(u + 1) * IDX_DIM].astype(BF16)
    wt_ref[...] = iw_ref[0].T[0:SUBLANES]

    q_pos = t0 + lax.broadcasted_iota(I32, (tk, tq), 1)
    k_off = lax.broadcasted_iota(I32, (tk, tq), 0)
    q_pos8 = t0 + lax.broadcasted_iota(I32, (SUBLANES, tq), 1)
    sub8 = lax.broadcasted_iota(I32, (SUBLANES, tq), 0)
    groups_per_block = tk // (WORD_BITS * SUBLANES)

    @pl.when((pl.program_id(0) == 0) & (i == 0))
    def _():
        plane_ref[...] = jnp.zeros(plane_ref.shape, I32)
        eye = (lax.broadcasted_iota(I32, (tq, tq), 0) == lax.broadcasted_iota(I32, (tq, tq), 1)).astype(F32)
        for h in range(NH):
            qt_ref[2 * DH:, h * tq:(h + 1) * tq] = eye.astype(BF16)

    def score_keys(j):
        k0 = pl.multiple_of(j * tk, tk)
        ikb = ik_ref[0, pl.ds(k0, tk), :]
        sc = jnp.zeros((tk, tq), F32)
        for c in range(IDX_HEADS // 2):
            lg = _dot(ikb, qit_ref[:, (2 * c) * tq:(2 * c + 2) * tq])
            for u in range(2):
                h = 2 * c + u
                sc = sc + jnp.maximum(lg[:, u * tq:(u + 1) * tq], 0.0) * wt_ref[h:h + 1, :]
        sc = sc + 0.0
        bits = pltpu.bitcast(sc, I32)
        key = bits ^ ((bits >> 31) & 0x7FFFFFFF)
        key_ref[j] = jnp.where(k0 + k_off <= q_pos, key, INT_MIN)

    def bit_planes(j):
        k0 = j * tk
        ukey = key_ref[j] ^ INT_MIN
        for g in range(groups_per_block):
            words = [ukey[(g * WORD_BITS + w) * SUBLANES:(g * WORD_BITS + w + 1) * SUBLANES, :]
                     for w in range(WORD_BITS)]
            planes = _bit_transpose32(words)
            r0 = pl.multiple_of((j * groups_per_block + g) * SUBLANES, SUBLANES)
            for p in range(WORD_BITS):
                plane_ref[p, pl.ds(r0, SUBLANES), :] = planes[p]
            n_adm = ((q_pos8 - (k0 + g * WORD_BITS * SUBLANES) - sub8) >> _log2(SUBLANES)) + 1
            n_adm = jnp.clip(n_adm, 0, WORD_BITS)
            cand_ref[pl.ds(r0, SUBLANES), :] = jnp.where(
                n_adm == 0, 0, lax.shift_left(jnp.int32(-1), WORD_BITS - jnp.maximum(n_adm, 1)))

    cand_ref[...] = jnp.zeros(cand_ref.shape, I32)
    last = nkb - 1
    score_keys(0)

    def score_step(j, carry):
        bit_planes(j)
        score_keys(jnp.minimum(j + 1, last))
        return carry

    lax.fori_loop(0, nkb, score_step, 0)

    def popcount_rows(words):
        pc = lax.population_count(words)
        parts = [pc[r * SUBLANES:(r + 1) * SUBLANES, :] for r in range(words.shape[0] // SUBLANES)]
        while len(parts) > 1:
            parts = [a + b for a, b in zip(parts[0::2], parts[1::2])]
        return jnp.sum(parts[0].astype(F32), axis=0, keepdims=True)

    def select_bit(p, carry):
        thr_u, rank = carry
        plane = plane_ref[p, 0:cand_ref.shape[0], :]
        cand = cand_ref[...]
        ones = cand & plane
        n_ones = popcount_rows(ones)
        take = n_ones >= rank
        thr_u = thr_u | jnp.where(take, lax.shift_left(jnp.int32(1), WORD_BITS - 1 - p), 0)
        rank = jnp.where(take, rank, rank - n_ones)
        cand_ref[...] = cand & (plane ^ jnp.where(take, 0, -1))
        return thr_u, rank

    thr_u, rank = lax.fori_loop(0, WORD_BITS, select_bit,
                                (jnp.zeros((1, tq), I32), jnp.full((1, tq), top_k, F32)))
    thr = thr_u ^ INT_MIN
    n_tied = popcount_rows(cand_ref[...])

    thr = jnp.maximum(thr, INT_MIN + 1)
    excess = jnp.max(jnp.where(n_tied > rank, 1.0, 0.0))

    @pl.when(excess > 0.0)
    def _():
        nbits = max(1, int(math.ceil(math.log2(key_ref.shape[0] * tk))))
        tied = cand_ref[...]
        row = lax.broadcasted_iota(I32, tied.shape, 0)
        first_key = _div_pow2(row, SUBLANES) * (WORD_BITS * SUBLANES) + _mod_pow2(row, SUBLANES)

        def count_tied_le(idx_lim):
            n_words = jnp.clip(((idx_lim - first_key) >> _log2(SUBLANES)) + 1, 0, WORD_BITS)
            upto = jnp.where(n_words == 0, 0, lax.shift_left(jnp.int32(-1), WORD_BITS - jnp.maximum(n_words, 1)))
            return popcount_rows(tied & upto)

        def idx_bit(it, lo):
            trial = lo + lax.shift_left(jnp.int32(1), nbits - 1 - it)
            return jnp.where(count_tied_le(trial - 1) < rank, trial, lo)

        cut = lax.fori_loop(0, nbits, idx_bit, jnp.zeros((1, tq), I32))

        def drop(j, carry):
            key = key_ref[j]
            key_ref[j] = jnp.where((key == thr) & (j * tk + k_off > cut), INT_MIN, key)
            return carry

        lax.fori_loop(0, nkb, drop, 0)

    m_ref[...] = jnp.full(m_ref.shape, NEG_BIG, F32)
    acc_ref[...] = jnp.zeros(acc_ref.shape, F32)
    cw = 2 * tq
    rc = 4 * SUBLANES
    ones_rows = jnp.ones((acc_ref.shape[0] - DH, tk), BF16)

    def logits(j, slot, live):
        k0 = pl.multiple_of(j * tk, tk)
        kvb = kv_ref[0, pl.ds(k0, tk), :]
        floor = jnp.where(live, jnp.int32(INT_MIN), jnp.int32(INT_MAX))
        bias = jnp.where(key_ref[j] >= jnp.maximum(thr, floor), 0.0, NEG_BIG)
        lhs = jnp.concatenate([kvb, bias.astype(BF16)], axis=1)
        for c in range(NH // 2):
            cols = slice(c * cw, (c + 1) * cw)
            s = _dot(lhs, qt_ref[:, cols])
            s_ref[slot, :, cols] = s
            smax_ref[slot, :, cols] = jnp.max(s, axis=0, keepdims=True)

    def softmax_values(j, slot):
        m_old = m_ref[...]
        m_new = jnp.maximum(m_old, smax_ref[slot])
        alpha = jnp.exp2(m_old - m_new)
        m_ref[...] = m_new
        vt1 = jnp.concatenate([kvt_ref[0, j, DH:, :], ones_rows], axis=0)
        for c in range(NH // 2):
            cols = slice(c * cw, (c + 1) * cw)
            m_rows = jnp.broadcast_to(m_new[:, cols], (rc, cw))
            for r in range(tk // rc):
                p = jnp.exp2(s_ref[slot, r * rc:(r + 1) * rc, cols] - m_rows)
                p_ref[r * rc:(r + 1) * rc, cols] = p.astype(BF16)
            acc_ref[:, cols] = acc_ref[:, cols] * alpha[:, cols] + _dot(vt1, p_ref[:, cols])

    logits(0, 0, True)

    def attend_pair(jj, carry):
        j0 = 2 * jj
        logits(jnp.minimum(j0 + 1, last), 1, j0 + 1 <= last)
        softmax_values(j0, 0)
        logits(jnp.minimum(j0 + 2, last), 0, j0 + 2 <= last)
        softmax_values(jnp.minimum(j0 + 1, last), 1)
        return carry

    lax.fori_loop(0, (nkb + 1) // 2, attend_pair, 0)

    for p in range(NH // 2):
        pair_t = jnp.concatenate(
            [acc_ref[0:DH, (2 * p + u) * tq:(2 * p + u + 1) * tq]
             / acc_ref[DH:DH + 1, (2 * p + u) * tq:(2 * p + u + 1) * tq]
             for u in range(2)], axis=0)
        o_ref[0, :, p * LANES:(p + 1) * LANES] = pair_t.T.astype(o_ref.dtype)


def _dsa(dq, dkv, dkvt, iq, ik, iw, tq, tk):
    B, L, _ = dq.shape
    top_k = min(DSA_TOPK_MAX, L // 4)
    nkb_max = L // tk
    assert dkvt.shape == (B, nkb_max, dkv.shape[2], tk)
    return pl.pallas_call(
        functools.partial(_dsa_kernel, tq=tq, tk=tk, top_k=float(top_k)),
        grid=(B, L // tq),
        in_specs=[pl.BlockSpec((1, tq, dq.shape[2]), lambda b, i: (b, i, 0)),
                  pl.BlockSpec((1, L, dkv.shape[2]), lambda b, i: (b, 0, 0)),
                  pl.BlockSpec((1,) + dkvt.shape[1:], lambda b, i: (b, 0, 0, 0)),
                  pl.BlockSpec((1, tq, iq.shape[2]), lambda b, i: (b, i, 0)),
                  pl.BlockSpec((1, L, ik.shape[2]), lambda b, i: (b, 0, 0)),
                  pl.BlockSpec((1, tq, iw.shape[2]), lambda b, i: (b, i, 0))],
        out_specs=pl.BlockSpec((1, tq, dq.shape[2]), lambda b, i: (b, i, 0)),
        out_shape=jax.ShapeDtypeStruct(dq.shape, BF16),
        scratch_shapes=[pltpu.VMEM((nkb_max, tk, tq), I32),
                        pltpu.VMEM((WORD_BITS, L // WORD_BITS + SUBLANES, tq), I32),
                        pltpu.VMEM((L // WORD_BITS, tq), I32),
                        pltpu.VMEM((2 * DSA_HEAD_DIM + tq, DSA_HEADS * tq), BF16),
                        pltpu.VMEM((LANES, IDX_HEADS * tq), BF16),
                        pltpu.VMEM((SUBLANES, tq), F32),
                        pltpu.VMEM((1, DSA_HEADS * tq), F32),
                        pltpu.VMEM((DSA_HEAD_DIM + 2 * SUBLANES, DSA_HEADS * tq), F32),
                        pltpu.VMEM((2, tk, DSA_HEADS * tq), F32),
                        pltpu.VMEM((2, 1, DSA_HEADS * tq), F32),
                        pltpu.VMEM((tk, DSA_HEADS * tq), BF16)],
        compiler_params=_cparams(("arbitrary", "arbitrary")),
        name="dsa",
    )(dq, dkv, dkvt, iq, ik, iw)


def _layer_norm(z, gain, bias):
    mu = jnp.mean(z, axis=-1, keepdims=True)
    zc = z - mu
    var = jnp.mean(zc * zc, axis=-1, keepdims=True)
    return zc * lax.rsqrt(var + LN_EPS) * gain + bias


def _merge_kernel(x_ref, yg_ref, yd_ref, wgate_ref, wbr_ref, wo_ref, gain_ref, bias_ref, o_ref, *, alpha):
    x = x_ref[0]
    D = x.shape[1]
    gates = jax.nn.sigmoid(_dot(x.astype(BF16), wgate_ref[...]))
    pg = _dot(yg_ref[0], wbr_ref[0])
    pd = _dot(yd_ref[0], wbr_ref[1])
    m = gates[:, 0:D] * pg + gates[:, D:] * pd
    mixed = _dot(m.astype(BF16), wo_ref[...])
    o_ref[0] = _layer_norm(alpha * x + mixed, gain_ref[...], bias_ref[...])


def _merge(x, yg, yd, wgate, wbr, wo, gain, bias, rows, alpha):
    B, L, D = x.shape

    def full(a):
        return pl.BlockSpec(a.shape, lambda b, i, _n=a.ndim: (0,) * _n)

    def rowblk(w):
        return pl.BlockSpec((1, rows, w), lambda b, i: (b, i, 0))

    return pl.pallas_call(
        functools.partial(_merge_kernel, alpha=alpha),
        grid=(B, L // rows),
        in_specs=[rowblk(D), rowblk(yg.shape[2]), rowblk(yd.shape[2]),
                  full(wgate), full(wbr), full(wo), full(gain), full(bias)],
        out_specs=rowblk(D),
        out_shape=jax.ShapeDtypeStruct((B, L, D), F32),
        compiler_params=_cparams(("arbitrary", "arbitrary")),
        name="merge_ln1",
    )(x, yg, yd, wgate, wbr, wo, gain, bias)


def _ffn_kernel(x_ref, wup_ref, cw_ref, cb_ref, wdn_ref, gain_ref, bias_ref, o_ref, hist_ref, u_ref, act_ref,
                *, alpha, n_chunks, cw, rsub):
    rows = x_ref.shape[1]
    H = SUBLANES

    @pl.when(pl.program_id(1) == 0)
    def _():
        hist_ref[...] = jnp.zeros_like(hist_ref)

    xb = x_ref[0].astype(BF16)

    def up_proj(c, slot):
        u_ref[slot, 0:H, :] = hist_ref[c]
        u_ref[slot, H:, :] = _dot(xb, wup_ref[c])
        hist_ref[c] = u_ref[slot, rows:rows + H, :]

    def conv_gate(c, slot):
        w = cw_ref[c]
        b = cb_ref[c]
        for r in range(rows // rsub):
            base = H + r * rsub
            cu = (b + u_ref[slot, base - 2:base - 2 + rsub, :] * w[0:1, :]
                  + u_ref[slot, base - 1:base - 1 + rsub, :] * w[1:2, :]
                  + u_ref[slot, base:base + rsub, :] * w[2:3, :])
            ug = cu[:, 0:cw]
            act = (ug * jax.nn.sigmoid(ug)) * cu[:, cw:]
            act_ref[c, r * rsub:(r + 1) * rsub, :] = act.astype(BF16)

    up_proj(0, 0)
    for c in range(1, n_chunks):
        up_proj(c, c % 2)
        conv_gate(c - 1, (c - 1) % 2)
    conv_gate(n_chunks - 1, (n_chunks - 1) % 2)
    act = jnp.concatenate([act_ref[c] for c in range(n_chunks)], axis=1)
    f = _dot(act, wdn_ref[...])
    o_ref[0] = _layer_norm(alpha * x_ref[0] + f, gain_ref[...], bias_ref[...])


def _ffn(x, wup, cwt, cbs, wdn, gain, bias, rows, alpha):
    B, L, D = x.shape
    n_chunks, _, cw2 = wup.shape

    def full(a):
        return pl.BlockSpec(a.shape, lambda b, i, _n=a.ndim: (0,) * _n, pipeline_mode=pl.Buffered(1))

    rowblk = pl.BlockSpec((1, rows, D), lambda b, i: (b, i, 0))
    return pl.pallas_call(
        functools.partial(_ffn_kernel, alpha=alpha, n_chunks=n_chunks, cw=cw2 // 2, rsub=min(rows, 64)),
        grid=(B, L // rows),
        in_specs=[rowblk, full(wup), full(cwt), full(cbs), full(wdn), full(gain), full(bias)],
        out_specs=rowblk,
        out_shape=jax.ShapeDtypeStruct((B, L, D), F32),
        scratch_shapes=[pltpu.VMEM((n_chunks, SUBLANES, cw2), F32),
                        pltpu.VMEM((2, rows + SUBLANES, cw2), F32),
                        pltpu.VMEM((n_chunks, rows, cw2 // 2), BF16)],
        compiler_params=_cparams(("arbitrary", "arbitrary")),
        name="ffn_ln2",
    )(x, wup, cwt, cbs, wdn, gain, bias)


def _pad_cols(w, width):
    return jnp.pad(w, ((0, 0), (0, width - w.shape[1])))


def _ffn_chunk_width(d_ff):
    for cw in (512, 384, 256, 128):
        if d_ff % cw == 0:
            return cw
    raise ValueError(f"d_ff={d_ff} must be a multiple of {LANES}")


def _token_mixers(x, w_in, w_a2, b_a, gla_gain):
    B, L, D = x.shape
    nqk = GLA_HEADS * GLA_DK
    nv = GLA_HEADS * GLA_DV
    ndq = DSA_HEADS * DSA_HEAD_DIM
    niq = IDX_HEADS * IDX_DIM
    widths = (nqk, nqk, nv, GLA_GATE_RANK, nv, ndq, DSA_HEAD_DIM, DSA_HEAD_DIM, niq, IDX_DIM, IDX_HEADS, N_BRANCH * D)
    offs = np.concatenate([[0], np.cumsum(widths)])
    assert w_in.shape == (D, int(offs[-1]))
    col = lambda n: w_in[:, int(offs[n]):int(offs[n + 1])]
    w_gq, w_gk, w_gv, w_ga, w_gr, w_dq, w_dk, w_dv, w_iq, w_ik, w_iw, w_gate = [col(n) for n in range(12)]
    wts = (
        jnp.concatenate([w_gq, w_gk, w_gv, w_gr], axis=1).astype(BF16),
        _pad_cols(w_ga, LANES).astype(BF16),
        jnp.pad(w_a2, ((0, LANES - GLA_GATE_RANK), (0, 0))).astype(BF16),
        b_a.reshape(1, nqk).astype(F32),
        w_dq.astype(BF16),
        jnp.concatenate([w_dk, w_dv], axis=1).astype(BF16),
        w_iq.astype(BF16),
        _pad_cols(w_ik, LANES).astype(BF16),
        _pad_cols(w_iw, LANES).astype(BF16),
    )
    def lane_freqs(head_dim):
        rot = head_dim // ROPE_FRACTION
        half = rot // 2
        inv = ROPE_THETA ** (-jnp.arange(half, dtype=F32) * 2.0 / rot)
        m = np.arange(LANES) % head_dim
        return jnp.where(jnp.asarray(m < rot), inv[jnp.asarray(m % half)], 0.0)

    invf = jnp.stack([lane_freqs(DSA_HEAD_DIM), lane_freqs(IDX_DIM)]).reshape(2, 1, LANES)
    tabs = _rope_tables(L, invf)

    tq = LANES
    tk = min(L, 512)
    gla_in, g, dq, dkv, dkvt, iq, ik, iw = _in_proj(x, wts, tabs, rows=tk)
    y_gla = _gla(gla_in, g, gla_gain.reshape(1, nv).astype(F32), rows=min(L, 512))
    y_dsa = _dsa(dq, dkv, dkvt, iq, ik, iw, tq, tk)
    return y_gla, y_dsa, w_gate


def _layer(x, w_in, w_a2, b_a, gla_gain, w_branch, w_o, ln1_g, ln1_b, w_up, conv_w, conv_b, w_down, ln2_g, ln2_b,
           depth):
    B, L, D = x.shape
    alpha = (2.0 * depth) ** 0.25
    y_gla, y_dsa, w_gate = _token_mixers(x, w_in, w_a2, b_a, gla_gain)
    x1 = _merge(x, y_gla, y_dsa, w_gate.astype(BF16), w_branch.astype(BF16), w_o.astype(BF16),
                ln1_g.reshape(1, D), ln1_b.reshape(1, D), rows=min(L, 512), alpha=alpha)

    d_ff = w_down.shape[0]
    cw = _ffn_chunk_width(d_ff)
    nch = d_ff // cw
    chunked = lambda a: jnp.concatenate(
        [a[..., :d_ff].reshape(a.shape[:-1] + (nch, cw)), a[..., d_ff:].reshape(a.shape[:-1] + (nch, cw))], axis=-1)
    wup = jnp.moveaxis(chunked(w_up), 1, 0).astype(BF16)
    cwt = jnp.moveaxis(chunked(conv_w), 1, 0)
    cwt = jnp.pad(cwt, ((0, 0), (0, SUBLANES - CONV_WIDTH), (0, 0)))
    cbs = jnp.moveaxis(chunked(conv_b[None, :]), 1, 0)
    return _ffn(x1, wup, cwt, cbs, w_down.astype(BF16), ln2_g.reshape(1, D), ln2_b.reshape(1, D),
                rows=min(L, 512), alpha=alpha)


def kernel(x, w_in, w_gla_a2, b_gla_a, gla_norm_gain, w_branch, w_o, ln1_gain, ln1_bias,
           w_up, conv_w, conv_b, w_down, ln2_gain, ln2_bias):
    depth = w_in.shape[0]
    for layer in range(depth):
        x = _layer(x, w_in[layer], w_gla_a2[layer], b_gla_a[layer], gla_norm_gain[layer], w_branch[layer],
                   w_o[layer], ln1_gain[layer], ln1_bias[layer], w_up[layer], conv_w[layer], conv_b[layer],
                   w_down[layer], ln2_gain[layer], ln2_bias[layer], depth)
    return x
```

```python
import functools
import math

import numpy as np
import jax
import jax.numpy as jnp
from jax import lax
from jax.experimental import pallas as pl
from jax.experimental.pallas import tpu as pltpu

F32 = jnp.float32
BF16 = jnp.bfloat16
I32 = jnp.int32

GLA_HEADS = 4
GLA_DK = 64
GLA_DV = 128
GLA_GATE_RANK = 16
GLA_TAU = 16.0
DSA_HEADS = 8
DSA_HEAD_DIM = 64
IDX_HEADS = 8
IDX_DIM = 32
DSA_TOPK_MAX = 256
ROPE_THETA = 500000.0
ROPE_FRACTION = 4
N_BRANCH = 2
BRANCH_WIDTH = 512
CONV_WIDTH = 3
LN_EPS = 1e-5
RMS_EPS = 1e-6

LANES = 128
SUBLANES = 8
VMEM_LIMIT_BYTES = 56 * 1024 * 1024

GLA_CHUNK = 64
GLA_SUB = 16
WORD_BITS = 32
INT_MIN = -(2 ** 31)
INT_MAX = 2 ** 31 - 1
NEG_BIG = -1e30
LOG2_E = 1.4426950408889634


def _cparams(sem):
    return pltpu.CompilerParams(dimension_semantics=sem, vmem_limit_bytes=VMEM_LIMIT_BYTES)


def _log2(n):
    assert n > 0 and n & (n - 1) == 0, f"{n} must be a power of two"
    return n.bit_length() - 1


def _div_pow2(x, n):
    return lax.shift_right_logical(x, jnp.int32(_log2(n)))


def _mod_pow2(x, n):
    _log2(n)
    return x & (n - 1)


def _dot(a, b):
    return jnp.dot(a, b, preferred_element_type=F32)


def _dot_nt(a, b):
    return lax.dot_general(a, b, (((1,), (1,)), ((), ())), preferred_element_type=F32)


def _dot_tn(a, b):
    return lax.dot_general(a, b, (((0,), (0,)), ((), ())), preferred_element_type=F32)


def _rope_table_kernel(invf_ref, c_ref, sa_ref, sb_ref, *, rows, periods, halves):
    t0 = pl.program_id(0) * rows
    pos = (t0 + lax.broadcasted_iota(I32, (rows, LANES), 0)).astype(F32)
    lane = lax.broadcasted_iota(I32, (rows, LANES), 1)
    for p, (period, half) in enumerate(zip(periods, halves)):
        ang = pos * invf_ref[p]
        c = jnp.cos(ang)
        s = jnp.sin(ang)
        m = _mod_pow2(lane, period)
        is1 = m < half
        is2 = (m >= half) & (m < 2 * half)
        c_ref[p] = jnp.where(is1 | is2, c, 1.0)
        sa_ref[p] = jnp.where(is1, -s, 0.0)
        sb_ref[p] = jnp.where(is2, s, 0.0)


def _rope_tables(L, invf):
    rows = min(L, 512)
    n = invf.shape[0]
    spec = pl.BlockSpec((n, rows, LANES), lambda i: (0, i, 0))
    shp = jax.ShapeDtypeStruct((n, L, LANES), F32)
    return pl.pallas_call(
        functools.partial(_rope_table_kernel, rows=rows, periods=(DSA_HEAD_DIM, IDX_DIM),
                          halves=(DSA_HEAD_DIM // ROPE_FRACTION // 2, IDX_DIM // ROPE_FRACTION // 2)),
        grid=(L // rows,),
        in_specs=[pl.BlockSpec((n, 1, LANES), lambda i: (0, 0, 0))],
        out_specs=[spec, spec, spec],
        out_shape=[shp, shp, shp],
        compiler_params=_cparams(("arbitrary",)),
        name="rope_tables",
    )(invf)


def _apply_rope(x, c, sa, sb, half, lane_limit=None):
    outs = []
    for g in range(x.shape[1] // LANES):
        xs = x[:, g * LANES:(g + 1) * LANES]
        up = pltpu.roll(xs, LANES - half, axis=1)
        dn = pltpu.roll(xs, half, axis=1)
        y = xs * c + up * sa + dn * sb
        if lane_limit is not None:
            lane = lax.broadcasted_iota(I32, xs.shape, 1)
            y = jnp.where(lane < lane_limit, y, xs)
        outs.append(y)
    return outs[0] if len(outs) == 1 else jnp.concatenate(outs, axis=1)


def _in_proj_kernel(x_ref, wgla_ref, wga_ref, wa2_ref, ba_ref, wdq_ref, wdkv_ref, wiq_ref, wik_ref, wiw_ref,
                    c_ref, sa_ref, sb_ref,
                    gla_ref, g_ref, dq_ref, dkv_ref, dkvt_ref, iq_ref, ik_ref, iw_ref):
    xb = x_ref[0].astype(BF16)
    hg = _dot(xb, wgla_ref[...])
    nqk = GLA_HEADS * GLA_DK
    gla_ref[0, :, 0:nqk] = (hg[:, 0:nqk] * (GLA_DK ** -0.5)).astype(BF16)
    gla_ref[0, :, nqk:] = hg[:, nqk:].astype(BF16)
    a_low = _dot(xb, wga_ref[...])
    z = _dot(a_low.astype(BF16), wa2_ref[...]) + ba_ref[...]
    g_ref[0] = (jnp.minimum(z, 0.0) - jnp.log1p(jnp.exp(-jnp.abs(z)))) * (1.0 / GLA_TAU)
    c64, sa64, sb64 = c_ref[0], sa_ref[0], sb_ref[0]
    c32, sa32, sb32 = c_ref[1], sa_ref[1], sb_ref[1]
    h64 = DSA_HEAD_DIM // ROPE_FRACTION // 2
    h32 = IDX_DIM // ROPE_FRACTION // 2
    dq = _dot(xb, wdq_ref[...])
    dq_ref[0] = (_apply_rope(dq, c64, sa64, sb64, h64) * (DSA_HEAD_DIM ** -0.5 * LOG2_E)).astype(BF16)
    dkv = _apply_rope(_dot(xb, wdkv_ref[...]), c64, sa64, sb64, h64, lane_limit=DSA_HEAD_DIM)
    dkv_ref[0] = dkv.astype(BF16)
    dkvt_ref[0, 0] = dkv.T.astype(BF16)
    iq = _dot(xb, wiq_ref[...])
    iq_ref[0] = _apply_rope(iq, c32, sa32, sb32, h32).astype(BF16)
    ik = _dot(xb, wik_ref[...])
    ik_ref[0] = _apply_rope(ik, c32, sa32, sb32, h32).astype(BF16)
    iw_ref[0] = _dot(xb, wiw_ref[...]) * ((IDX_HEADS ** -0.5) * (IDX_DIM ** -0.5))


def _in_proj(x, wts, tabs, rows):
    B, L, D = x.shape
    wgla, wga, wa2, ba, wdq, wdkv, wiq, wik, wiw = wts
    c, sa, sb = tabs
    n_l = L // rows

    def full(a):
        return pl.BlockSpec(a.shape, lambda i, b, _n=a.ndim: (0,) * _n)

    def rowblk(w):
        return pl.BlockSpec((1, rows, w), lambda i, b: (b, i, 0))

    tab = pl.BlockSpec((2, rows, LANES), lambda i, b: (0, i, 0))
    widths = (wgla.shape[1], GLA_HEADS * GLA_DK, wdq.shape[1], wdkv.shape[1], wiq.shape[1], wik.shape[1], wiw.shape[1])
    dtypes = (BF16, F32, BF16, BF16, BF16, BF16, F32)
    out_specs = [rowblk(w) for w in widths]
    out_shape = [jax.ShapeDtypeStruct((B, L, w), dt) for w, dt in zip(widths, dtypes)]
    out_specs.insert(4, pl.BlockSpec((1, 1, wdkv.shape[1], rows), lambda i, b: (b, i, 0, 0)))
    out_shape.insert(4, jax.ShapeDtypeStruct((B, n_l, wdkv.shape[1], rows), BF16))
    return pl.pallas_call(
        _in_proj_kernel,
        grid=(n_l, B),
        in_specs=[rowblk(D)] + [full(w) for w in wts] + [tab, tab, tab],
        out_specs=out_specs,
        out_shape=out_shape,
        compiler_params=_cparams(("arbitrary", "arbitrary")),
        name="in_proj",
    )(x, *wts, c, sa, sb)


def _gla_kernel(qkvr_ref, g_ref, gain_ref, tril_ref, expand_ref, y_ref, st_ref, *, rows):
    C, S = GLA_CHUNK, GLA_SUB
    H, DK, DV = GLA_HEADS, GLA_DK, GLA_DV
    HK, HV = H * DK, H * DV

    @pl.when(pl.program_id(1) == 0)
    def _():
        st_ref[...] = jnp.zeros_like(st_ref)

    ri = lax.broadcasted_iota(I32, (C, HK), 0)
    li = lax.broadcasted_iota(I32, (C, HK), 1)
    head_of_lane = _div_pow2(li, DK)
    row_minus_col = ri - _mod_pow2(li, DK)
    sr = _div_pow2(lax.broadcasted_iota(I32, (HV, HK), 0), DV)
    scol = _div_pow2(lax.broadcasted_iota(I32, (HV, HK), 1), DK)
    st_mask = sr == scol
    tril = tril_ref[...]
    expand = expand_ref[...]
    gain = gain_ref[...]

    def chunk(ci, carry):
        r0 = pl.multiple_of(ci * C, C)
        blk = qkvr_ref[0, pl.ds(r0, C), :]
        q = blk[:, 0:HK].astype(F32)
        k = blk[:, HK:2 * HK].astype(F32)
        v = blk[:, 2 * HK:2 * HK + HV].astype(F32)
        r = blk[:, 2 * HK + HV:].astype(F32)
        g = g_ref[0, pl.ds(r0, C), :]
        g1 = g.astype(BF16)
        rem = g - g1.astype(F32)
        g2 = rem.astype(BF16)
        g3 = (rem - g2.astype(F32)).astype(BF16)
        b = _dot(tril, g1) + _dot(tril, g2) + _dot(tril, g3)
        b_last = b[C - 1:C, :]
        st = st_ref[...]
        o = _dot_nt((q * jnp.exp(b)).astype(BF16), st.astype(BF16))
        p_acc = jnp.zeros((H * C, C), F32)
        for I in range(1, C // S):
            beta = b[I * S - 1:I * S, :]
            in_I = (ri >= I * S) & (ri < (I + 1) * S)
            qI = jnp.where(in_I, q * jnp.exp(jnp.minimum(b - beta, 0.0)), 0.0)
            kI = jnp.where(ri < I * S, k * jnp.exp(jnp.minimum(beta - b, 0.0)), 0.0)
            q_st = jnp.concatenate([jnp.where(head_of_lane == h, qI, 0.0) for h in range(H)], axis=0)
            p_acc = p_acc + _dot_nt(q_st.astype(BF16), kI.astype(BF16))
        eg = jnp.exp(g)
        p_diag = jnp.zeros((C, HK), F32)
        e_d = None
        GRP = 4
        for d0 in range(0, S, GRP):
            ts = []
            for d in range(d0, d0 + GRP):
                if d == 0:
                    ts.append(q * k)
                else:
                    e_d = eg if d == 1 else e_d * pltpu.roll(eg, d - 1, axis=0)
                    kd = pltpu.roll(k, d, axis=0)
                    ts.append(jnp.where(_mod_pow2(ri, S) >= d, q * kd * e_d, 0.0))
            s_all = _dot(jnp.concatenate(ts, axis=0).astype(BF16), expand)
            for n, d in enumerate(range(d0, d0 + GRP)):
                p_diag = p_diag + jnp.where(row_minus_col == d, s_all[n * C:(n + 1) * C, :], 0.0)
        vb = v.astype(BF16)
        o_intra = jnp.concatenate(
            [_dot((p_acc[h * C:(h + 1) * C, :] + p_diag[:, h * DK:(h + 1) * DK]).astype(BF16),
                  vb[:, h * DV:(h + 1) * DV]) for h in range(H)], axis=1)
        o = o + o_intra
        k_end = k * jnp.exp(b_last - b)
        upd = _dot_tn(vb, k_end.astype(BF16))
        st_ref[...] = st * jnp.exp(b_last) + jnp.where(st_mask, upd, 0.0)
        outs = []
        for h in range(H):
            oh = o[:, h * DV:(h + 1) * DV]
            ms = jnp.mean(oh * oh, axis=-1, keepdims=True)
            outs.append(oh * lax.rsqrt(ms + RMS_EPS))
        on = jnp.concatenate(outs, axis=1) * gain
        y = on * (r * jax.nn.sigmoid(r))
        y_ref[0, pl.ds(r0, C), :] = y.astype(y_ref.dtype)
        return carry

    lax.fori_loop(0, rows // C, chunk, 0, unroll=4)


def _gla(qkvr, g, gain, rows):
    B, L, W = qkvr.shape
    HK, HV = GLA_HEADS * GLA_DK, GLA_HEADS * GLA_DV
    C = GLA_CHUNK
    tril = jnp.asarray(np.tril(np.ones((C, C), np.float32)), BF16)
    expand = jnp.asarray(np.kron(np.eye(GLA_HEADS, dtype=np.float32), np.ones((GLA_DK, GLA_DK), np.float32)), BF16)
    return pl.pallas_call(
        functools.partial(_gla_kernel, rows=rows),
        grid=(B, L // rows),
        in_specs=[pl.BlockSpec((1, rows, W), lambda b, i: (b, i, 0)),
                  pl.BlockSpec((1, rows, HK), lambda b, i: (b, i, 0)),
                  pl.BlockSpec((1, HV), lambda b, i: (0, 0)),
                  pl.BlockSpec((C, C), lambda b, i: (0, 0)),
                  pl.BlockSpec((HK, HK), lambda b, i: (0, 0))],
        out_specs=pl.BlockSpec((1, rows, HV), lambda b, i: (b, i, 0)),
        out_shape=jax.ShapeDtypeStruct((B, L, HV), BF16),
        scratch_shapes=[pltpu.VMEM((HV, HK), F32)],
        compiler_params=_cparams(("arbitrary", "arbitrary")),
        name="gla",
    )(qkvr, g, gain, tril, expand)


def _bit_transpose32(words):
    a = list(words)
    assert len(a) == WORD_BITS
    j, m = WORD_BITS // 2, 0x0000FFFF
    while j:
        m_i32 = m - (1 << 32) if m >= (1 << 31) else m
        for k in range(WORD_BITS):
            if k & j == 0:
                t = (a[k] ^ lax.shift_right_logical(a[k + j], jnp.int32(j))) & m_i32
                a[k] = a[k] ^ t
                a[k + j] = a[k + j] ^ lax.shift_left(t, jnp.int32(j))
        j >>= 1
        m = (m ^ (m << j)) & 0xFFFFFFFF
    return a


def _dsa_kernel(dq_ref, kv_ref, kvt_ref, iq_ref, ik_ref, iw_ref, o_ref,
                key_ref, plane_ref, cand_ref, sel_thr_ref, sel_rank_ref, qt_ref, qit_ref, wt_ref, m_ref, acc_ref,
                s_ref, smax_ref, p_ref, *, tq, tk, top_k):
    NH, DH = DSA_HEADS, DSA_HEAD_DIM
    assert tq == LANES
    i = pl.program_id(1)
    t0 = i * tq
    nkb = (t0 + tq + tk - 1) // tk

    qt_ref[DH:2 * DH, :] = jnp.zeros((DH, NH * tq), BF16)
    for p in range(NH // 2):
        pair_t = dq_ref[0, :, p * LANES:(p + 1) * LANES].astype(F32).T
        qt_ref[0:DH, (2 * p) * tq:(2 * p + 1) * tq] = pair_t[0:DH].astype(BF16)
        qt_ref[0:DH, (2 * p + 1) * tq:(2 * p + 2) * tq] = pair_t[DH:].astype(BF16)
    qit_ref[...] = jnp.zeros(qit_ref.shape, BF16)
    heads_per_group = LANES // IDX_DIM
    for g in range(IDX_HEADS // heads_per_group):
        grp_t = iq_ref[0, :, g * LANES:(g + 1) * LANES].astype(F32).T
        for u in range(heads_per_group):
            h = g * heads_per_group + u
            qit_ref[0:IDX_DIM, h * tq:(h + 1) * tq] = grp_t[u * IDX_DIM:(u + 1) * IDX_DIM].astype(BF16)
    wt_ref[...] = iw_ref[0].T[0:SUBLANES]

    q_row = t0 + lax.broadcasted_iota(I32, (1, tq), 1)
    k_off = lax.broadcasted_iota(I32, (tk, tq), 0)
    q_pos8 = t0 + lax.broadcasted_iota(I32, (SUBLANES, tq), 1)
    sub8 = lax.broadcasted_iota(I32, (SUBLANES, tq), 0)
    groups_per_block = tk // (WORD_BITS * SUBLANES)

    @pl.when((pl.program_id(0) == 0) & (i == 0))
    def _():
        plane_ref[...] = jnp.zeros(plane_ref.shape, I32)
        eye = (lax.broadcasted_iota(I32, (tq, tq), 0) == lax.broadcasted_iota(I32, (tq, tq), 1)).astype(F32)
        for h in range(NH):
            qt_ref[2 * DH:, h * tq:(h + 1) * tq] = eye.astype(BF16)

    def score_keys(j):
        k0 = pl.multiple_of(j * tk, tk)
        ikb = ik_ref[0, pl.ds(k0, tk), :]
        sc = jnp.zeros((tk, tq), F32)
        for c in range(IDX_HEADS // 2):
            lg = _dot(ikb, qit_ref[:, (2 * c) * tq:(2 * c + 2) * tq])
            for u in range(2):
                h = 2 * c + u
                sc = sc + jnp.maximum(lg[:, u * tq:(u + 1) * tq], 0.0) * wt_ref[h:h + 1, :]
        sc = sc + 0.0
        bits = pltpu.bitcast(sc, I32)
        key = bits ^ ((bits >> 31) & 0x7FFFFFFF)
        key_ref[j] = jnp.where(k_off <= q_row - k0, key, INT_MIN)

    def bit_planes(j):
        k0 = j * tk
        key = key_ref[j]
        for g in range(groups_per_block):
            words = [key[(g * WORD_BITS + w) * SUBLANES:(g * WORD_BITS + w + 1) * SUBLANES, :]
                     for w in range(WORD_BITS)]
            planes = _bit_transpose32(words)
            planes[0] = ~planes[0]
            r0 = pl.multiple_of((j * groups_per_block + g) * SUBLANES, SUBLANES)
            for p in range(WORD_BITS):
                plane_ref[p, pl.ds(r0, SUBLANES), :] = planes[p]
            n_adm = ((q_pos8 - (k0 + g * WORD_BITS * SUBLANES) - sub8) >> _log2(SUBLANES)) + 1
            n_adm = jnp.clip(n_adm, 0, WORD_BITS)
            cand_ref[pl.ds(r0, SUBLANES), :] = jnp.where(
                n_adm == 0, 0, lax.shift_left(jnp.int32(-1), WORD_BITS - jnp.maximum(n_adm, 1)))

    cand_ref[...] = jnp.zeros(cand_ref.shape, I32)
    last = nkb - 1
    score_keys(0)

    def score_step(j, carry):
        bit_planes(j)
        score_keys(jnp.minimum(j + 1, last))
        return carry

    lax.fori_loop(0, nkb, score_step, 0)

    def popcount_rows(words):
        pc = lax.population_count(words)
        parts = [pc[r * SUBLANES:(r + 1) * SUBLANES, :] for r in range(words.shape[0] // SUBLANES)]
        while len(parts) > 1:
            odd = parts[-1:] if len(parts) % 2 else []
            parts = [a + b for a, b in zip(parts[0::2], parts[1::2])] + odd
        return jnp.sum(parts[0].astype(F32), axis=0, keepdims=True)

    def radix_select(n_rows):
        def select_bit(p, carry):
            thr_u, rank = carry
            plane = plane_ref[p, 0:n_rows, :]
            cand = cand_ref[0:n_rows, :]
            ones = cand & plane
            n_ones = popcount_rows(ones)
            take = n_ones >= rank
            thr_u = thr_u | jnp.where(take, lax.shift_left(jnp.int32(1), WORD_BITS - 1 - p), 0)
            rank = jnp.where(take, rank, rank - n_ones)
            cand_ref[0:n_rows, :] = cand & (plane ^ jnp.where(take, 0, -1))
            return thr_u, rank

        thr_u, rank = lax.fori_loop(0, WORD_BITS, select_bit,
                                    (jnp.zeros((1, tq), I32), jnp.full((1, tq), top_k, F32)))
        sel_thr_ref[...] = thr_u
        sel_rank_ref[...] = rank

    n_classes = min(4, cand_ref.shape[0] // SUBLANES)
    class_rows = cand_ref.shape[0] // n_classes
    rows_used = nkb * groups_per_block * SUBLANES
    for cls in range(1, n_classes + 1):
        pl.when((rows_used > (cls - 1) * class_rows) & (rows_used <= cls * class_rows))(
            functools.partial(radix_select, cls * class_rows))
    thr_u, rank = sel_thr_ref[...], sel_rank_ref[...]
    thr = thr_u ^ INT_MIN
    n_tied = popcount_rows(cand_ref[...])

    thr = jnp.maximum(thr, INT_MIN + 1)
    excess = jnp.max(jnp.where(n_tied > rank, 1.0, 0.0))

    @pl.when(excess > 0.0)
    def _():
        nbits = max(1, int(math.ceil(math.log2(key_ref.shape[0] * tk))))
        tied = cand_ref[...]
        row = lax.broadcasted_iota(I32, tied.shape, 0)
        first_key = _div_pow2(row, SUBLANES) * (WORD_BITS * SUBLANES) + _mod_pow2(row, SUBLANES)

        def count_tied_le(idx_lim):
            n_words = jnp.clip(((idx_lim - first_key) >> _log2(SUBLANES)) + 1, 0, WORD_BITS)
            upto = jnp.where(n_words == 0, 0, lax.shift_left(jnp.int32(-1), WORD_BITS - jnp.maximum(n_words, 1)))
            return popcount_rows(tied & upto)

        def idx_bit(it, lo):
            trial = lo + lax.shift_left(jnp.int32(1), nbits - 1 - it)
            return jnp.where(count_tied_le(trial - 1) < rank, trial, lo)

        cut = lax.fori_loop(0, nbits, idx_bit, jnp.zeros((1, tq), I32))

        def drop(j, carry):
            key = key_ref[j]
            key_ref[j] = jnp.where((key == thr) & (j * tk + k_off > cut), INT_MIN, key)
            return carry

        lax.fori_loop(0, nkb, drop, 0)

    m_ref[...] = jnp.full(m_ref.shape, NEG_BIG, F32)
    acc_ref[...] = jnp.zeros(acc_ref.shape, F32)
    cw = 2 * tq
    rc = 4 * SUBLANES
    ones_rows = jnp.ones((acc_ref.shape[0] - DH, tk), BF16)

    def logits(j, slot, live):
        k0 = pl.multiple_of(j * tk, tk)
        kvb = kv_ref[0, pl.ds(k0, tk), :]
        floor = jnp.where(live, jnp.int32(INT_MIN), jnp.int32(INT_MAX))
        bias = jnp.where(key_ref[j] >= jnp.maximum(thr, floor), 0.0, NEG_BIG)
        lhs = jnp.concatenate([kvb, bias.astype(BF16)], axis=1)
        for c in range(NH // 2):
            cols = slice(c * cw, (c + 1) * cw)
            s = _dot(lhs, qt_ref[:, cols])
            s_ref[slot, :, cols] = s
            smax_ref[slot, :, cols] = jnp.max(s, axis=0, keepdims=True)

    def softmax_values(j, slot):
        m_old = m_ref[...]
        m_new = jnp.maximum(m_old, smax_ref[slot])
        alpha = jnp.exp2(m_old - m_new)
        m_ref[...] = m_new
        vt1 = jnp.concatenate([kvt_ref[0, j, DH:, :], ones_rows], axis=0)
        for c in range(NH // 2):
            cols = slice(c * cw, (c + 1) * cw)
            m_rows = jnp.broadcast_to(m_new[:, cols], (rc, cw))
            for r in range(tk // rc):
                p = jnp.exp2(s_ref[slot, r * rc:(r + 1) * rc, cols] - m_rows)
                p_ref[r * rc:(r + 1) * rc, cols] = p.astype(BF16)
            acc_ref[:, cols] = acc_ref[:, cols] * alpha[:, cols] + _dot(vt1, p_ref[:, cols])

    logits(0, 0, True)

    def attend_pair(jj, carry):
        j0 = 2 * jj
        logits(jnp.minimum(j0 + 1, last), 1, j0 + 1 <= last)
        softmax_values(j0, 0)
        logits(jnp.minimum(j0 + 2, last), 0, j0 + 2 <= last)
        softmax_values(jnp.minimum(j0 + 1, last), 1)
        return carry

    lax.fori_loop(0, (nkb + 1) // 2, attend_pair, 0)

    for p in range(NH // 2):
        pair_t = jnp.concatenate(
            [acc_ref[0:DH, (2 * p + u) * tq:(2 * p + u + 1) * tq]
             / acc_ref[DH:DH + 1, (2 * p + u) * tq:(2 * p + u + 1) * tq]
             for u in range(2)], axis=0)
        o_ref[0, :, p * LANES:(p + 1) * LANES] = pair_t.T.astype(o_ref.dtype)


def _dsa(dq, dkv, dkvt, iq, ik, iw, tq, tk):
    B, L, _ = dq.shape
    top_k = min(DSA_TOPK_MAX, L // 4)
    nkb_max = L // tk
    assert dkvt.shape == (B, nkb_max, dkv.shape[2], tk)
    return pl.pallas_call(
        functools.partial(_dsa_kernel, tq=tq, tk=tk, top_k=float(top_k)),
        grid=(B, L // tq),
        in_specs=[pl.BlockSpec((1, tq, dq.shape[2]), lambda b, i: (b, i, 0)),
                  pl.BlockSpec((1, L, dkv.shape[2]), lambda b, i: (b, 0, 0)),
                  pl.BlockSpec((1,) + dkvt.shape[1:], lambda b, i: (b, 0, 0, 0)),
                  pl.BlockSpec((1, tq, iq.shape[2]), lambda b, i: (b, i, 0)),
                  pl.BlockSpec((1, L, ik.shape[2]), lambda b, i: (b, 0, 0)),
                  pl.BlockSpec((1, tq, iw.shape[2]), lambda b, i: (b, i, 0))],
        out_specs=pl.BlockSpec((1, tq, dq.shape[2]), lambda b, i: (b, i, 0)),
        out_shape=jax.ShapeDtypeStruct(dq.shape, BF16),
        scratch_shapes=[pltpu.VMEM((nkb_max, tk, tq), I32),
                        pltpu.VMEM((WORD_BITS, L // WORD_BITS + SUBLANES, tq), I32),
                        pltpu.VMEM((L // WORD_BITS, tq), I32),
                        pltpu.VMEM((1, tq), I32),
                        pltpu.VMEM((1, tq), F32),
                        pltpu.VMEM((2 * DSA_HEAD_DIM + tq, DSA_HEADS * tq), BF16),
                        pltpu.VMEM((LANES, IDX_HEADS * tq), BF16),
                        pltpu.VMEM((SUBLANES, tq), F32),
                        pltpu.VMEM((1, DSA_HEADS * tq), F32),
                        pltpu.VMEM((DSA_HEAD_DIM + 2 * SUBLANES, DSA_HEADS * tq), F32),
                        pltpu.VMEM((2, tk, DSA_HEADS * tq), F32),
                        pltpu.VMEM((2, 1, DSA_HEADS * tq), F32),
                        pltpu.VMEM((tk, DSA_HEADS * tq), BF16)],
        compiler_params=_cparams(("arbitrary", "arbitrary")),
        name="dsa",
    )(dq, dkv, dkvt, iq, ik, iw)


def _layer_norm(z, gain, bias):
    mu = jnp.mean(z, axis=-1, keepdims=True)
    zc = z - mu
    var = jnp.mean(zc * zc, axis=-1, keepdims=True)
    return zc * lax.rsqrt(var + LN_EPS) * gain + bias


def _merge_kernel(x_ref, yg_ref, yd_ref, wgate_ref, wbr_ref, wo_ref, gain_ref, bias_ref, o_ref, *, alpha):
    x = x_ref[0]
    D = x.shape[1]
    gates = jax.nn.sigmoid(_dot(x.astype(BF16), wgate_ref[...]))
    pg = _dot(yg_ref[0], wbr_ref[0])
    pd = _dot(yd_ref[0], wbr_ref[1])
    m = gates[:, 0:D] * pg + gates[:, D:] * pd
    mixed = _dot(m.astype(BF16), wo_ref[...])
    o_ref[0] = _layer_norm(alpha * x + mixed, gain_ref[...], bias_ref[...])


def _merge(x, yg, yd, wgate, wbr, wo, gain, bias, rows, alpha):
    B, L, D = x.shape

    def full(a):
        return pl.BlockSpec(a.shape, lambda b, i, _n=a.ndim: (0,) * _n)

    def rowblk(w):
        return pl.BlockSpec((1, rows, w), lambda b, i: (b, i, 0))

    return pl.pallas_call(
        functools.partial(_merge_kernel, alpha=alpha),
        grid=(B, L // rows),
        in_specs=[rowblk(D), rowblk(yg.shape[2]), rowblk(yd.shape[2]),
                  full(wgate), full(wbr), full(wo), full(gain), full(bias)],
        out_specs=rowblk(D),
        out_shape=jax.ShapeDtypeStruct((B, L, D), F32),
        compiler_params=_cparams(("arbitrary", "arbitrary")),
        name="merge_ln1",
    )(x, yg, yd, wgate, wbr, wo, gain, bias)


def _ffn_kernel(x_ref, wup_ref, cw_ref, cb_ref, wdn_ref, gain_ref, bias_ref, o_ref, hist_ref, u_ref, act_ref,
                *, alpha, n_chunks, cw, rsub):
    rows = x_ref.shape[1]
    H = SUBLANES

    @pl.when(pl.program_id(1) == 0)
    def _():
        hist_ref[...] = jnp.zeros_like(hist_ref)

    xb = x_ref[0].astype(BF16)

    def up_proj(c, slot):
        u_ref[slot, 0:H, :] = hist_ref[c]
        u_ref[slot, H:, :] = _dot(xb, wup_ref[c])
        hist_ref[c] = u_ref[slot, rows:rows + H, :]

    def conv_gate(c, slot):
        w = cw_ref[c]
        b = cb_ref[c]
        for r in range(rows // rsub):
            base = H + r * rsub
            cu = (b + u_ref[slot, base - 2:base - 2 + rsub, :] * w[0:1, :]
                  + u_ref[slot, base - 1:base - 1 + rsub, :] * w[1:2, :]
                  + u_ref[slot, base:base + rsub, :] * w[2:3, :])
            ug = cu[:, 0:cw]
            act = (ug * jax.nn.sigmoid(ug)) * cu[:, cw:]
            act_ref[c, r * rsub:(r + 1) * rsub, :] = act.astype(BF16)

    up_proj(0, 0)
    for c in range(1, n_chunks):
        up_proj(c, c % 2)
        conv_gate(c - 1, (c - 1) % 2)
    conv_gate(n_chunks - 1, (n_chunks - 1) % 2)
    act = jnp.concatenate([act_ref[c] for c in range(n_chunks)], axis=1)
    f = _dot(act, wdn_ref[...])
    o_ref[0] = _layer_norm(alpha * x_ref[0] + f, gain_ref[...], bias_ref[...])


def _ffn(x, wup, cwt, cbs, wdn, gain, bias, rows, alpha):
    B, L, D = x.shape
    n_chunks, _, cw2 = wup.shape

    def full(a):
        return pl.BlockSpec(a.shape, lambda b, i, _n=a.ndim: (0,) * _n, pipeline_mode=pl.Buffered(1))

    rowblk = pl.BlockSpec((1, rows, D), lambda b, i: (b, i, 0))
    return pl.pallas_call(
        functools.partial(_ffn_kernel, alpha=alpha, n_chunks=n_chunks, cw=cw2 // 2, rsub=min(rows, 64)),
        grid=(B, L // rows),
        in_specs=[rowblk, full(wup), full(cwt), full(cbs), full(wdn), full(gain), full(bias)],
        out_specs=rowblk,
        out_shape=jax.ShapeDtypeStruct((B, L, D), F32),
        scratch_shapes=[pltpu.VMEM((n_chunks, SUBLANES, cw2), F32),
                        pltpu.VMEM((2, rows + SUBLANES, cw2), F32),
                        pltpu.VMEM((n_chunks, rows, cw2 // 2), BF16)],
        compiler_params=_cparams(("arbitrary", "arbitrary")),
        name="ffn_ln2",
    )(x, wup, cwt, cbs, wdn, gain, bias)


def _pad_cols(w, width):
    return jnp.pad(w, ((0, 0), (0, width - w.shape[1])))


def _ffn_chunk_width(d_ff):
    for cw in (512, 384, 256, 128):
        if d_ff % cw == 0:
            return cw
    raise ValueError(f"d_ff={d_ff} must be a multiple of {LANES}")


def _token_mixers(x, w_in, w_a2, b_a, gla_gain):
    B, L, D = x.shape
    nqk = GLA_HEADS * GLA_DK
    nv = GLA_HEADS * GLA_DV
    ndq = DSA_HEADS * DSA_HEAD_DIM
    niq = IDX_HEADS * IDX_DIM
    widths = (nqk, nqk, nv, GLA_GATE_RANK, nv, ndq, DSA_HEAD_DIM, DSA_HEAD_DIM, niq, IDX_DIM, IDX_HEADS, N_BRANCH * D)
    offs = np.concatenate([[0], np.cumsum(widths)])
    assert w_in.shape == (D, int(offs[-1]))
    col = lambda n: w_in[:, int(offs[n]):int(offs[n + 1])]
    w_gq, w_gk, w_gv, w_ga, w_gr, w_dq, w_dk, w_dv, w_iq, w_ik, w_iw, w_gate = [col(n) for n in range(12)]
    wts = (
        jnp.concatenate([w_gq, w_gk, w_gv, w_gr], axis=1).astype(BF16),
        _pad_cols(w_ga, LANES).astype(BF16),
        jnp.pad(w_a2, ((0, LANES - GLA_GATE_RANK), (0, 0))).astype(BF16),
        b_a.reshape(1, nqk).astype(F32),
        w_dq.astype(BF16),
        jnp.concatenate([w_dk, w_dv], axis=1).astype(BF16),
        w_iq.astype(BF16),
        _pad_cols(w_ik, LANES).astype(BF16),
        _pad_cols(w_iw, LANES).astype(BF16),
    )
    def lane_freqs(head_dim):
        rot = head_dim // ROPE_FRACTION
        half = rot // 2
        inv = ROPE_THETA ** (-jnp.arange(half, dtype=F32) * 2.0 / rot)
        m = np.arange(LANES) % head_dim
        return jnp.where(jnp.asarray(m < rot), inv[jnp.asarray(m % half)], 0.0)

    invf = jnp.stack([lane_freqs(DSA_HEAD_DIM), lane_freqs(IDX_DIM)]).reshape(2, 1, LANES)
    tabs = _rope_tables(L, invf)

    tq = LANES
    tk = min(L, 512)
    gla_in, g, dq, dkv, dkvt, iq, ik, iw = _in_proj(x, wts, tabs, rows=tk)
    y_gla = _gla(gla_in, g, gla_gain.reshape(1, nv).astype(F32), rows=min(L, 512))
    y_dsa = _dsa(dq, dkv, dkvt, iq, ik, iw, tq, tk)
    return y_gla, y_dsa, w_gate


def _layer(x, w_in, w_a2, b_a, gla_gain, w_branch, w_o, ln1_g, ln1_b, w_up, conv_w, conv_b, w_down, ln2_g, ln2_b,
           depth):
    B, L, D = x.shape
    alpha = (2.0 * depth) ** 0.25
    y_gla, y_dsa, w_gate = _token_mixers(x, w_in, w_a2, b_a, gla_gain)
    x1 = _merge(x, y_gla, y_dsa, w_gate.astype(BF16), w_branch.astype(BF16), w_o.astype(BF16),
                ln1_g.reshape(1, D), ln1_b.reshape(1, D), rows=min(L, 512), alpha=alpha)

    d_ff = w_down.shape[0]
    cw = _ffn_chunk_width(d_ff)
    nch = d_ff // cw
    chunked = lambda a: jnp.concatenate(
        [a[..., :d_ff].reshape(a.shape[:-1] + (nch, cw)), a[..., d_ff:].reshape(a.shape[:-1] + (nch, cw))], axis=-1)
    wup = jnp.moveaxis(chunked(w_up), 1, 0).astype(BF16)
    cwt = jnp.moveaxis(chunked(conv_w), 1, 0)
    cwt = jnp.pad(cwt, ((0, 0), (0, SUBLANES - CONV_WIDTH), (0, 0)))
    cbs = jnp.moveaxis(chunked(conv_b[None, :]), 1, 0)
    return _ffn(x1, wup, cwt, cbs, w_down.astype(BF16), ln2_g.reshape(1, D), ln2_b.reshape(1, D),
                rows=min(L, 512), alpha=alpha)


def kernel(x, w_in, w_gla_a2, b_gla_a, gla_norm_gain, w_branch, w_o, ln1_gain, ln1_bias,
           w_up, conv_w, conv_b, w_down, ln2_gain, ln2_bias):
    depth = w_in.shape[0]
    for layer in range(depth):
        x = _layer(x, w_in[layer], w_gla_a2[layer], b_gla_a[layer], gla_norm_gain[layer], w_branch[layer],
                   w_o[layer], ln1_gain[layer], ln1_bias[layer], w_up[layer], conv_w[layer], conv_b[layer],
                   w_down[layer], ln2_gain[layer], ln2_bias[layer], depth)
    return x
```

```python
import functools
import math

import numpy as np
import jax
import jax.numpy as jnp
from jax import lax
from jax.experimental import pallas as pl
from jax.experimental.pallas import tpu as pltpu

F32 = jnp.float32
BF16 = jnp.bfloat16
I32 = jnp.int32

GLA_HEADS = 4
GLA_DK = 64
GLA_DV = 128
GLA_GATE_RANK = 16
GLA_TAU = 16.0
DSA_HEADS = 8
DSA_HEAD_DIM = 64
IDX_HEADS = 8
IDX_DIM = 32
DSA_TOPK_MAX = 256
ROPE_THETA = 500000.0
ROPE_FRACTION = 4
N_BRANCH = 2
BRANCH_WIDTH = 512
CONV_WIDTH = 3
LN_EPS = 1e-5
RMS_EPS = 1e-6

LANES = 128
SUBLANES = 8
VMEM_LIMIT_BYTES = 56 * 1024 * 1024

GLA_CHUNK = 64
GLA_SUB = 16
WORD_BITS = 32
INT_MIN = -(2 ** 31)
INT_MAX = 2 ** 31 - 1
NEG_BIG = -1e30
LOG2_E = 1.4426950408889634


def _cparams(sem):
    return pltpu.CompilerParams(dimension_semantics=sem, vmem_limit_bytes=VMEM_LIMIT_BYTES)


def _log2(n):
    assert n > 0 and n & (n - 1) == 0, f"{n} must be a power of two"
    return n.bit_length() - 1


def _div_pow2(x, n):
    return lax.shift_right_logical(x, jnp.int32(_log2(n)))


def _mod_pow2(x, n):
    _log2(n)
    return x & (n - 1)


def _dot(a, b):
    return jnp.dot(a, b, preferred_element_type=F32)


def _dot_nt(a, b):
    return lax.dot_general(a, b, (((1,), (1,)), ((), ())), preferred_element_type=F32)


def _dot_tn(a, b):
    return lax.dot_general(a, b, (((0,), (0,)), ((), ())), preferred_element_type=F32)


def _rope_table_kernel(invf_ref, c_ref, sa_ref, sb_ref, *, rows, periods, halves):
    t0 = pl.program_id(0) * rows
    pos = (t0 + lax.broadcasted_iota(I32, (rows, LANES), 0)).astype(F32)
    lane = lax.broadcasted_iota(I32, (rows, LANES), 1)
    for p, (period, half) in enumerate(zip(periods, halves)):
        ang = pos * invf_ref[p]
        c = jnp.cos(ang)
        s = jnp.sin(ang)
        m = _mod_pow2(lane, period)
        is1 = m < half
        is2 = (m >= half) & (m < 2 * half)
        c_ref[p] = jnp.where(is1 | is2, c, 1.0)
        sa_ref[p] = jnp.where(is1, -s, 0.0)
        sb_ref[p] = jnp.where(is2, s, 0.0)


def _rope_tables(L, invf):
    rows = min(L, 512)
    n = invf.shape[0]
    spec = pl.BlockSpec((n, rows, LANES), lambda i: (0, i, 0))
    shp = jax.ShapeDtypeStruct((n, L, LANES), F32)
    return pl.pallas_call(
        functools.partial(_rope_table_kernel, rows=rows, periods=(DSA_HEAD_DIM, IDX_DIM),
                          halves=(DSA_HEAD_DIM // ROPE_FRACTION // 2, IDX_DIM // ROPE_FRACTION // 2)),
        grid=(L // rows,),
        in_specs=[pl.BlockSpec((n, 1, LANES), lambda i: (0, 0, 0))],
        out_specs=[spec, spec, spec],
        out_shape=[shp, shp, shp],
        compiler_params=_cparams(("arbitrary",)),
        name="rope_tables",
    )(invf)


def _apply_rope(x, c, sa, sb, half, lane_limit=None):
    outs = []
    for g in range(x.shape[1] // LANES):
        xs = x[:, g * LANES:(g + 1) * LANES]
        up = pltpu.roll(xs, LANES - half, axis=1)
        dn = pltpu.roll(xs, half, axis=1)
        y = xs * c + up * sa + dn * sb
        if lane_limit is not None:
            lane = lax.broadcasted_iota(I32, xs.shape, 1)
            y = jnp.where(lane < lane_limit, y, xs)
        outs.append(y)
    return outs[0] if len(outs) == 1 else jnp.concatenate(outs, axis=1)


def _in_proj_kernel(x_ref, wgla_ref, wga_ref, wa2_ref, ba_ref, wdq_ref, wdkv_ref, wiq_ref, wik_ref, wiw_ref,
                    c_ref, sa_ref, sb_ref,
                    gla_ref, g_ref, dq_ref, dkv_ref, dkvt_ref, iq_ref, ik_ref, iw_ref):
    xb = x_ref[0].astype(BF16)
    hg = _dot(xb, wgla_ref[...])
    nqk = GLA_HEADS * GLA_DK
    gla_ref[0, :, 0:nqk] = (hg[:, 0:nqk] * (GLA_DK ** -0.5)).astype(BF16)
    gla_ref[0, :, nqk:] = hg[:, nqk:].astype(BF16)
    a_low = _dot(xb, wga_ref[...])
    z = _dot(a_low.astype(BF16), wa2_ref[...]) + ba_ref[...]
    g_ref[0] = (jnp.minimum(z, 0.0) - jnp.log1p(jnp.exp(-jnp.abs(z)))) * (1.0 / GLA_TAU)
    c64, sa64, sb64 = c_ref[0], sa_ref[0], sb_ref[0]
    c32, sa32, sb32 = c_ref[1], sa_ref[1], sb_ref[1]
    h64 = DSA_HEAD_DIM // ROPE_FRACTION // 2
    h32 = IDX_DIM // ROPE_FRACTION // 2
    dq = _dot(xb, wdq_ref[...])
    dq_ref[0] = (_apply_rope(dq, c64, sa64, sb64, h64) * (DSA_HEAD_DIM ** -0.5 * LOG2_E)).astype(BF16)
    dkv = _apply_rope(_dot(xb, wdkv_ref[...]), c64, sa64, sb64, h64, lane_limit=DSA_HEAD_DIM)
    dkv_ref[0] = dkv.astype(BF16)
    dkvt_ref[0, 0] = dkv.T.astype(BF16)
    iq = _dot(xb, wiq_ref[...])
    iq_ref[0] = _apply_rope(iq, c32, sa32, sb32, h32).astype(BF16)
    ik = _dot(xb, wik_ref[...])
    ik_ref[0] = _apply_rope(ik, c32, sa32, sb32, h32).astype(BF16)
    iw_ref[0] = _dot(xb, wiw_ref[...]) * ((IDX_HEADS ** -0.5) * (IDX_DIM ** -0.5))


def _in_proj(x, wts, tabs, rows):
    B, L, D = x.shape
    wgla, wga, wa2, ba, wdq, wdkv, wiq, wik, wiw = wts
    c, sa, sb = tabs
    n_l = L // rows

    def full(a):
        return pl.BlockSpec(a.shape, lambda i, b, _n=a.ndim: (0,) * _n)

    def rowblk(w):
        return pl.BlockSpec((1, rows, w), lambda i, b: (b, i, 0))

    tab = pl.BlockSpec((2, rows, LANES), lambda i, b: (0, i, 0))
    widths = (wgla.shape[1], GLA_HEADS * GLA_DK, wdq.shape[1], wdkv.shape[1], wiq.shape[1], wik.shape[1], wiw.shape[1])
    dtypes = (BF16, F32, BF16, BF16, BF16, BF16, F32)
    out_specs = [rowblk(w) for w in widths]
    out_shape = [jax.ShapeDtypeStruct((B, L, w), dt) for w, dt in zip(widths, dtypes)]
    out_specs.insert(4, pl.BlockSpec((1, 1, wdkv.shape[1], rows), lambda i, b: (b, i, 0, 0)))
    out_shape.insert(4, jax.ShapeDtypeStruct((B, n_l, wdkv.shape[1], rows), BF16))
    return pl.pallas_call(
        _in_proj_kernel,
        grid=(n_l, B),
        in_specs=[rowblk(D)] + [full(w) for w in wts] + [tab, tab, tab],
        out_specs=out_specs,
        out_shape=out_shape,
        compiler_params=_cparams(("arbitrary", "arbitrary")),
        name="in_proj",
    )(x, *wts, c, sa, sb)


def _gla_kernel(qkvr_ref, g_ref, gain_ref, tril_ref, expand_ref, y_ref, st_ref, *, rows):
    C, S = GLA_CHUNK, GLA_SUB
    H, DK, DV = GLA_HEADS, GLA_DK, GLA_DV
    HK, HV = H * DK, H * DV

    @pl.when(pl.program_id(1) == 0)
    def _():
        st_ref[...] = jnp.zeros_like(st_ref)

    ri = lax.broadcasted_iota(I32, (C, HK), 0)
    li = lax.broadcasted_iota(I32, (C, HK), 1)
    head_of_lane = _div_pow2(li, DK)
    row_minus_col = ri - _mod_pow2(li, DK)
    sr = _div_pow2(lax.broadcasted_iota(I32, (HV, HK), 0), DV)
    scol = _div_pow2(lax.broadcasted_iota(I32, (HV, HK), 1), DK)
    st_mask = sr == scol
    tril = tril_ref[...]
    expand = expand_ref[...]
    gain = gain_ref[...]

    def chunk(ci, carry):
        r0 = pl.multiple_of(ci * C, C)
        blk = qkvr_ref[0, pl.ds(r0, C), :]
        q = blk[:, 0:HK].astype(F32)
        k = blk[:, HK:2 * HK].astype(F32)
        v = blk[:, 2 * HK:2 * HK + HV].astype(F32)
        r = blk[:, 2 * HK + HV:].astype(F32)
        g = g_ref[0, pl.ds(r0, C), :]
        g1 = g.astype(BF16)
        rem = g - g1.astype(F32)
        g2 = rem.astype(BF16)
        g3 = (rem - g2.astype(F32)).astype(BF16)
        b = _dot(tril, g1) + _dot(tril, g2) + _dot(tril, g3)
        b_last = b[C - 1:C, :]
        st = st_ref[...]
        o = _dot_nt((q * jnp.exp(b)).astype(BF16), st.astype(BF16))
        p_acc = jnp.zeros((H * C, C), F32)
        for I in range(1, C // S):
            beta = b[I * S - 1:I * S, :]
            in_I = (ri >= I * S) & (ri < (I + 1) * S)
            qI = jnp.where(in_I, q * jnp.exp(jnp.minimum(b - beta, 0.0)), 0.0)
            kI = jnp.where(ri < I * S, k * jnp.exp(jnp.minimum(beta - b, 0.0)), 0.0)
            q_st = jnp.concatenate([jnp.where(head_of_lane == h, qI, 0.0) for h in range(H)], axis=0)
            p_acc = p_acc + _dot_nt(q_st.astype(BF16), kI.astype(BF16))
        eg = jnp.exp(g)
        p_diag = jnp.zeros((C, HK), F32)
        e_d = None
        GRP = 4
        for d0 in range(0, S, GRP):
            ts = []
            for d in range(d0, d0 + GRP):
                if d == 0:
                    ts.append(q * k)
                else:
                    e_d = eg if d == 1 else e_d * pltpu.roll(eg, d - 1, axis=0)
                    kd = pltpu.roll(k, d, axis=0)
                    ts.append(jnp.where(_mod_pow2(ri, S) >= d, q * kd * e_d, 0.0))
            s_all = _dot(jnp.concatenate(ts, axis=0).astype(BF16), expand)
            for n, d in enumerate(range(d0, d0 + GRP)):
                p_diag = p_diag + jnp.where(row_minus_col == d, s_all[n * C:(n + 1) * C, :], 0.0)
        vb = v.astype(BF16)
        o_intra = jnp.concatenate(
            [_dot((p_acc[h * C:(h + 1) * C, :] + p_diag[:, h * DK:(h + 1) * DK]).astype(BF16),
                  vb[:, h * DV:(h + 1) * DV]) for h in range(H)], axis=1)
        o = o + o_intra
        k_end = k * jnp.exp(b_last - b)
        upd = _dot_tn(vb, k_end.astype(BF16))
        st_ref[...] = st * jnp.exp(b_last) + jnp.where(st_mask, upd, 0.0)
        outs = []
        for h in range(H):
            oh = o[:, h * DV:(h + 1) * DV]
            ms = jnp.mean(oh * oh, axis=-1, keepdims=True)
            outs.append(oh * lax.rsqrt(ms + RMS_EPS))
        on = jnp.concatenate(outs, axis=1) * gain
        y = on * (r * jax.nn.sigmoid(r))
        y_ref[0, pl.ds(r0, C), :] = y.astype(y_ref.dtype)
        return carry

    lax.fori_loop(0, rows // C, chunk, 0, unroll=4)


def _gla(qkvr, g, gain, rows):
    B, L, W = qkvr.shape
    HK, HV = GLA_HEADS * GLA_DK, GLA_HEADS * GLA_DV
    C = GLA_CHUNK
    tril = jnp.asarray(np.tril(np.ones((C, C), np.float32)), BF16)
    expand = jnp.asarray(np.kron(np.eye(GLA_HEADS, dtype=np.float32), np.ones((GLA_DK, GLA_DK), np.float32)), BF16)
    return pl.pallas_call(
        functools.partial(_gla_kernel, rows=rows),
        grid=(B, L // rows),
        in_specs=[pl.BlockSpec((1, rows, W), lambda b, i: (b, i, 0)),
                  pl.BlockSpec((1, rows, HK), lambda b, i: (b, i, 0)),
                  pl.BlockSpec((1, HV), lambda b, i: (0, 0)),
                  pl.BlockSpec((C, C), lambda b, i: (0, 0)),
                  pl.BlockSpec((HK, HK), lambda b, i: (0, 0))],
        out_specs=pl.BlockSpec((1, rows, HV), lambda b, i: (b, i, 0)),
        out_shape=jax.ShapeDtypeStruct((B, L, HV), BF16),
        scratch_shapes=[pltpu.VMEM((HV, HK), F32)],
        compiler_params=_cparams(("arbitrary", "arbitrary")),
        name="gla",
    )(qkvr, g, gain, tril, expand)


def _bit_transpose32(words):
    a = list(words)
    assert len(a) == WORD_BITS
    j, m = WORD_BITS // 2, 0x0000FFFF
    while j:
        m_i32 = m - (1 << 32) if m >= (1 << 31) else m
        for k in range(WORD_BITS):
            if k & j == 0:
                t = (a[k] ^ lax.shift_right_logical(a[k + j], jnp.int32(j))) & m_i32
                a[k] = a[k] ^ t
                a[k + j] = a[k + j] ^ lax.shift_left(t, jnp.int32(j))
        j >>= 1
        m = (m ^ (m << j)) & 0xFFFFFFFF
    return a


def _dsa_kernel(dq_ref, kv_ref, kvt_ref, iq_ref, ik_ref, iw_ref, o_ref,
                key_ref, plane_ref, cand_ref, sel_thr_ref, sel_rank_ref, qt_ref, qit_ref, wt_ref, m_ref, acc_ref,
                s_ref, smax_ref, p_ref, *, tq, tk, top_k):
    NH, DH = DSA_HEADS, DSA_HEAD_DIM
    assert tq == LANES
    i = pl.program_id(1)
    t0 = i * tq
    nkb = (t0 + tq + tk - 1) // tk

    qt_ref[DH:2 * DH, :] = jnp.zeros((DH, NH * tq), BF16)
    for p in range(NH // 2):
        pair_t = dq_ref[0, :, p * LANES:(p + 1) * LANES].astype(F32).T
        qt_ref[0:DH, (2 * p) * tq:(2 * p + 1) * tq] = pair_t[0:DH].astype(BF16)
        qt_ref[0:DH, (2 * p + 1) * tq:(2 * p + 2) * tq] = pair_t[DH:].astype(BF16)
    qit_ref[...] = jnp.zeros(qit_ref.shape, BF16)
    heads_per_group = LANES // IDX_DIM
    for g in range(IDX_HEADS // heads_per_group):
        grp_t = iq_ref[0, :, g * LANES:(g + 1) * LANES].astype(F32).T
        for u in range(heads_per_group):
            h = g * heads_per_group + u
            qit_ref[0:IDX_DIM, h * tq:(h + 1) * tq] = grp_t[u * IDX_DIM:(u + 1) * IDX_DIM].astype(BF16)
    wt_ref[...] = iw_ref[0].T[0:SUBLANES]

    q_row = t0 + lax.broadcasted_iota(I32, (1, tq), 1)
    k_off = lax.broadcasted_iota(I32, (tk, tq), 0)
    q_pos8 = t0 + lax.broadcasted_iota(I32, (SUBLANES, tq), 1)
    sub8 = lax.broadcasted_iota(I32, (SUBLANES, tq), 0)
    groups_per_block = tk // (WORD_BITS * SUBLANES)

    @pl.when((pl.program_id(0) == 0) & (i == 0))
    def _():
        plane_ref[...] = jnp.zeros(plane_ref.shape, I32)
        eye = (lax.broadcasted_iota(I32, (tq, tq), 0) == lax.broadcasted_iota(I32, (tq, tq), 1)).astype(F32)
        for h in range(NH):
            qt_ref[2 * DH:, h * tq:(h + 1) * tq] = eye.astype(BF16)

    def score_keys(j):
        k0 = pl.multiple_of(j * tk, tk)
        ikb = ik_ref[0, pl.ds(k0, tk), :]
        sc = jnp.zeros((tk, tq), F32)
        for c in range(IDX_HEADS // 2):
            lg = _dot(ikb, qit_ref[:, (2 * c) * tq:(2 * c + 2) * tq])
            for u in range(2):
                h = 2 * c + u
                sc = sc + jnp.maximum(lg[:, u * tq:(u + 1) * tq], 0.0) * wt_ref[h:h + 1, :]
        sc = sc + 0.0
        bits = pltpu.bitcast(sc, I32)
        key = bits ^ ((bits >> 31) & 0x7FFFFFFF)
        key_ref[j] = jnp.where(k_off <= q_row - k0, key, INT_MIN)

    def bit_planes(j):
        k0 = j * tk
        key = key_ref[j]
        for g in range(groups_per_block):
            words = [key[(g * WORD_BITS + w) * SUBLANES:(g * WORD_BITS + w + 1) * SUBLANES, :]
                     for w in range(WORD_BITS)]
            planes = _bit_transpose32(words)
            planes[0] = ~planes[0]
            r0 = pl.multiple_of((j * groups_per_block + g) * SUBLANES, SUBLANES)
            for p in range(WORD_BITS):
                plane_ref[p, pl.ds(r0, SUBLANES), :] = planes[p]
            n_adm = ((q_pos8 - (k0 + g * WORD_BITS * SUBLANES) - sub8) >> _log2(SUBLANES)) + 1
            n_adm = jnp.clip(n_adm, 0, WORD_BITS)
            cand_ref[pl.ds(r0, SUBLANES), :] = jnp.where(
                n_adm == 0, 0, lax.shift_left(jnp.int32(-1), WORD_BITS - jnp.maximum(n_adm, 1)))

    cand_ref[...] = jnp.zeros(cand_ref.shape, I32)
    last = nkb - 1
    score_keys(0)

    def score_step(j, carry):
        bit_planes(j)
        score_keys(j + 1)
        return carry

    lax.fori_loop(0, last, score_step, 0)
    bit_planes(last)

    def popcount_rows(words):
        pc = lax.population_count(words)
        parts = [pc[r * SUBLANES:(r + 1) * SUBLANES, :] for r in range(words.shape[0] // SUBLANES)]
        while len(parts) > 1:
            odd = parts[-1:] if len(parts) % 2 else []
            parts = [a + b for a, b in zip(parts[0::2], parts[1::2])] + odd
        return jnp.sum(parts[0].astype(F32), axis=0, keepdims=True)

    def radix_select(n_rows):
        def select_bit(p, carry):
            thr_u, rank = carry
            plane = plane_ref[p, 0:n_rows, :]
            cand = cand_ref[0:n_rows, :]
            ones = cand & plane
            n_ones = popcount_rows(ones)
            take = n_ones >= rank
            thr_u = thr_u | jnp.where(take, lax.shift_left(jnp.int32(1), WORD_BITS - 1 - p), 0)
            rank = jnp.where(take, rank, rank - n_ones)
            cand_ref[0:n_rows, :] = cand & (plane ^ jnp.where(take, 0, -1))
            return thr_u, rank

        thr_u, rank = lax.fori_loop(0, WORD_BITS, select_bit,
                                    (jnp.zeros((1, tq), I32), jnp.full((1, tq), top_k, F32)))
        sel_thr_ref[...] = thr_u
        sel_rank_ref[...] = rank

    n_classes = min(4, cand_ref.shape[0] // SUBLANES)
    class_rows = cand_ref.shape[0] // n_classes
    rows_used = nkb * groups_per_block * SUBLANES
    for cls in range(1, n_classes + 1):
        pl.when((rows_used > (cls - 1) * class_rows) & (rows_used <= cls * class_rows))(
            functools.partial(radix_select, cls * class_rows))
    thr_u, rank = sel_thr_ref[...], sel_rank_ref[...]
    thr = thr_u ^ INT_MIN
    n_tied = popcount_rows(cand_ref[...])

    thr = jnp.maximum(thr, INT_MIN + 1)
    excess = jnp.max(jnp.where(n_tied > rank, 1.0, 0.0))

    @pl.when(excess > 0.0)
    def _():
        nbits = max(1, int(math.ceil(math.log2(key_ref.shape[0] * tk))))
        tied = cand_ref[...]
        row = lax.broadcasted_iota(I32, tied.shape, 0)
        first_key = _div_pow2(row, SUBLANES) * (WORD_BITS * SUBLANES) + _mod_pow2(row, SUBLANES)

        def count_tied_le(idx_lim):
            n_words = jnp.clip(((idx_lim - first_key) >> _log2(SUBLANES)) + 1, 0, WORD_BITS)
            upto = jnp.where(n_words == 0, 0, lax.shift_left(jnp.int32(-1), WORD_BITS - jnp.maximum(n_words, 1)))
            return popcount_rows(tied & upto)

        def idx_bit(it, lo):
            trial = lo + lax.shift_left(jnp.int32(1), nbits - 1 - it)
            return jnp.where(count_tied_le(trial - 1) < rank, trial, lo)

        cut = lax.fori_loop(0, nbits, idx_bit, jnp.zeros((1, tq), I32))

        def drop(j, carry):
            key = key_ref[j]
            key_ref[j] = jnp.where((key == thr) & (j * tk + k_off > cut), INT_MIN, key)
            return carry

        lax.fori_loop(0, nkb, drop, 0)

    m_ref[...] = jnp.full(m_ref.shape, NEG_BIG, F32)
    acc_ref[...] = jnp.zeros(acc_ref.shape, F32)
    cw = 2 * tq
    rc = 4 * SUBLANES
    ones_rows = jnp.ones((acc_ref.shape[0] - DH, tk), BF16)

    def logits(j, slot, live):
        k0 = pl.multiple_of(j * tk, tk)
        kvb = kv_ref[0, pl.ds(k0, tk), :]
        floor = jnp.where(live, jnp.int32(INT_MIN), jnp.int32(INT_MAX))
        bias = jnp.where(key_ref[j] >= jnp.maximum(thr, floor), 0.0, NEG_BIG)
        lhs = jnp.concatenate([kvb, bias.astype(BF16)], axis=1)
        for c in range(NH // 2):
            cols = slice(c * cw, (c + 1) * cw)
            s = _dot(lhs, qt_ref[:, cols])
            s_ref[slot, :, cols] = s
            smax_ref[slot, :, cols] = jnp.max(s, axis=0, keepdims=True)

    def softmax_values(j, slot):
        m_old = m_ref[...]
        m_new = jnp.maximum(m_old, smax_ref[slot])
        alpha = jnp.exp2(m_old - m_new)
        m_ref[...] = m_new
        vt1 = jnp.concatenate([kvt_ref[0, j, DH:, :], ones_rows], axis=0)
        for c in range(NH // 2):
            cols = slice(c * cw, (c + 1) * cw)
            m_rows = jnp.broadcast_to(m_new[:, cols], (rc, cw))
            for r in range(tk // rc):
                p = jnp.exp2(s_ref[slot, r * rc:(r + 1) * rc, cols] - m_rows)
                p_ref[r * rc:(r + 1) * rc, cols] = p.astype(BF16)
            acc_ref[:, cols] = acc_ref[:, cols] * alpha[:, cols] + _dot(vt1, p_ref[:, cols])

    logits(0, 0, True)

    def attend_pair(jj, carry):
        j0 = 2 * jj
        logits(jnp.minimum(j0 + 1, last), 1, j0 + 1 <= last)
        softmax_values(j0, 0)
        logits(jnp.minimum(j0 + 2, last), 0, j0 + 2 <= last)
        softmax_values(jnp.minimum(j0 + 1, last), 1)
        return carry

    lax.fori_loop(0, (nkb + 1) // 2, attend_pair, 0)

    for p in range(NH // 2):
        pair_t = jnp.concatenate(
            [acc_ref[0:DH, (2 * p + u) * tq:(2 * p + u + 1) * tq]
             / acc_ref[DH:DH + 1, (2 * p + u) * tq:(2 * p + u + 1) * tq]
             for u in range(2)], axis=0)
        o_ref[0, :, p * LANES:(p + 1) * LANES] = pair_t.T.astype(o_ref.dtype)


def _dsa(dq, dkv, dkvt, iq, ik, iw, tq, tk):
    B, L, _ = dq.shape
    top_k = min(DSA_TOPK_MAX, L // 4)
    nkb_max = L // tk
    assert dkvt.shape == (B, nkb_max, dkv.shape[2], tk)
    return pl.pallas_call(
        functools.partial(_dsa_kernel, tq=tq, tk=tk, top_k=float(top_k)),
        grid=(B, L // tq),
        in_specs=[pl.BlockSpec((1, tq, dq.shape[2]), lambda b, i: (b, i, 0)),
                  pl.BlockSpec((1, L, dkv.shape[2]), lambda b, i: (b, 0, 0)),
                  pl.BlockSpec((1,) + dkvt.shape[1:], lambda b, i: (b, 0, 0, 0)),
                  pl.BlockSpec((1, tq, iq.shape[2]), lambda b, i: (b, i, 0)),
                  pl.BlockSpec((1, L, ik.shape[2]), lambda b, i: (b, 0, 0)),
                  pl.BlockSpec((1, tq, iw.shape[2]), lambda b, i: (b, i, 0))],
        out_specs=pl.BlockSpec((1, tq, dq.shape[2]), lambda b, i: (b, i, 0)),
        out_shape=jax.ShapeDtypeStruct(dq.shape, BF16),
        scratch_shapes=[pltpu.VMEM((nkb_max, tk, tq), I32),
                        pltpu.VMEM((WORD_BITS, L // WORD_BITS + SUBLANES, tq), I32),
                        pltpu.VMEM((L // WORD_BITS, tq), I32),
                        pltpu.VMEM((1, tq), I32),
                        pltpu.VMEM((1, tq), F32),
                        pltpu.VMEM((2 * DSA_HEAD_DIM + tq, DSA_HEADS * tq), BF16),
                        pltpu.VMEM((LANES, IDX_HEADS * tq), BF16),
                        pltpu.VMEM((SUBLANES, tq), F32),
                        pltpu.VMEM((1, DSA_HEADS * tq), F32),
                        pltpu.VMEM((DSA_HEAD_DIM + 2 * SUBLANES, DSA_HEADS * tq), F32),
                        pltpu.VMEM((2, tk, DSA_HEADS * tq), F32),
                        pltpu.VMEM((2, 1, DSA_HEADS * tq), F32),
                        pltpu.VMEM((tk, DSA_HEADS * tq), BF16)],
        compiler_params=_cparams(("arbitrary", "arbitrary")),
        name="dsa",
    )(dq, dkv, dkvt, iq, ik, iw)


def _layer_norm(z, gain, bias):
    mu = jnp.mean(z, axis=-1, keepdims=True)
    zc = z - mu
    var = jnp.mean(zc * zc, axis=-1, keepdims=True)
    return zc * lax.rsqrt(var + LN_EPS) * gain + bias


def _merge_kernel(x_ref, yg_ref, yd_ref, wgate_ref, wbr_ref, wo_ref, gain_ref, bias_ref, o_ref, *, alpha):
    x = x_ref[0]
    D = x.shape[1]
    gates = jax.nn.sigmoid(_dot(x.astype(BF16), wgate_ref[...]))
    pg = _dot(yg_ref[0], wbr_ref[0])
    pd = _dot(yd_ref[0], wbr_ref[1])
    m = gates[:, 0:D] * pg + gates[:, D:] * pd
    mixed = _dot(m.astype(BF16), wo_ref[...])
    o_ref[0] = _layer_norm(alpha * x + mixed, gain_ref[...], bias_ref[...])


def _merge(x, yg, yd, wgate, wbr, wo, gain, bias, rows, alpha):
    B, L, D = x.shape

    def full(a):
        return pl.BlockSpec(a.shape, lambda b, i, _n=a.ndim: (0,) * _n)

    def rowblk(w):
        return pl.BlockSpec((1, rows, w), lambda b, i: (b, i, 0))

    return pl.pallas_call(
        functools.partial(_merge_kernel, alpha=alpha),
        grid=(B, L // rows),
        in_specs=[rowblk(D), rowblk(yg.shape[2]), rowblk(yd.shape[2]),
                  full(wgate), full(wbr), full(wo), full(gain), full(bias)],
        out_specs=rowblk(D),
        out_shape=jax.ShapeDtypeStruct((B, L, D), F32),
        compiler_params=_cparams(("arbitrary", "arbitrary")),
        name="merge_ln1",
    )(x, yg, yd, wgate, wbr, wo, gain, bias)


def _ffn_kernel(x_ref, wup_ref, cw_ref, cb_ref, wdn_ref, gain_ref, bias_ref, o_ref, hist_ref, u_ref, act_ref,
                *, alpha, n_chunks, cw, rsub):
    rows = x_ref.shape[1]
    H = SUBLANES

    @pl.when(pl.program_id(1) == 0)
    def _():
        hist_ref[...] = jnp.zeros_like(hist_ref)

    xb = x_ref[0].astype(BF16)

    def up_proj(c, slot):
        u_ref[slot, 0:H, :] = hist_ref[c]
        u_ref[slot, H:, :] = _dot(xb, wup_ref[c])
        hist_ref[c] = u_ref[slot, rows:rows + H, :]

    def conv_gate(c, slot):
        w = cw_ref[c]
        b = cb_ref[c]
        for r in range(rows // rsub):
            base = H + r * rsub
            cu = (b + u_ref[slot, base - 2:base - 2 + rsub, :] * w[0:1, :]
                  + u_ref[slot, base - 1:base - 1 + rsub, :] * w[1:2, :]
                  + u_ref[slot, base:base + rsub, :] * w[2:3, :])
            ug = cu[:, 0:cw]
            act = (ug * jax.nn.sigmoid(ug)) * cu[:, cw:]
            act_ref[c, r * rsub:(r + 1) * rsub, :] = act.astype(BF16)

    up_proj(0, 0)
    for c in range(1, n_chunks):
        up_proj(c, c % 2)
        conv_gate(c - 1, (c - 1) % 2)
    conv_gate(n_chunks - 1, (n_chunks - 1) % 2)
    act = jnp.concatenate([act_ref[c] for c in range(n_chunks)], axis=1)
    f = _dot(act, wdn_ref[...])
    o_ref[0] = _layer_norm(alpha * x_ref[0] + f, gain_ref[...], bias_ref[...])


def _ffn(x, wup, cwt, cbs, wdn, gain, bias, rows, alpha):
    B, L, D = x.shape
    n_chunks, _, cw2 = wup.shape

    def full(a):
        return pl.BlockSpec(a.shape, lambda b, i, _n=a.ndim: (0,) * _n, pipeline_mode=pl.Buffered(1))

    rowblk = pl.BlockSpec((1, rows, D), lambda b, i: (b, i, 0))
    return pl.pallas_call(
        functools.partial(_ffn_kernel, alpha=alpha, n_chunks=n_chunks, cw=cw2 // 2, rsub=min(rows, 64)),
        grid=(B, L // rows),
        in_specs=[rowblk, full(wup), full(cwt), full(cbs), full(wdn), full(gain), full(bias)],
        out_specs=rowblk,
        out_shape=jax.ShapeDtypeStruct((B, L, D), F32),
        scratch_shapes=[pltpu.VMEM((n_chunks, SUBLANES, cw2), F32),
                        pltpu.VMEM((2, rows + SUBLANES, cw2), F32),
                        pltpu.VMEM((n_chunks, rows, cw2 // 2), BF16)],
        compiler_params=_cparams(("arbitrary", "arbitrary")),
        name="ffn_ln2",
    )(x, wup, cwt, cbs, wdn, gain, bias)


def _pad_cols(w, width):
    return jnp.pad(w, ((0, 0), (0, width - w.shape[1])))


def _ffn_chunk_width(d_ff):
    for cw in (512, 384, 256, 128):
        if d_ff % cw == 0:
            return cw
    raise ValueError(f"d_ff={d_ff} must be a multiple of {LANES}")


def _token_mixers(x, w_in, w_a2, b_a, gla_gain):
    B, L, D = x.shape
    nqk = GLA_HEADS * GLA_DK
    nv = GLA_HEADS * GLA_DV
    ndq = DSA_HEADS * DSA_HEAD_DIM
    niq = IDX_HEADS * IDX_DIM
    widths = (nqk, nqk, nv, GLA_GATE_RANK, nv, ndq, DSA_HEAD_DIM, DSA_HEAD_DIM, niq, IDX_DIM, IDX_HEADS, N_BRANCH * D)
    offs = np.concatenate([[0], np.cumsum(widths)])
    assert w_in.shape == (D, int(offs[-1]))
    col = lambda n: w_in[:, int(offs[n]):int(offs[n + 1])]
    w_gq, w_gk, w_gv, w_ga, w_gr, w_dq, w_dk, w_dv, w_iq, w_ik, w_iw, w_gate = [col(n) for n in range(12)]
    wts = (
        jnp.concatenate([w_gq, w_gk, w_gv, w_gr], axis=1).astype(BF16),
        _pad_cols(w_ga, LANES).astype(BF16),
        jnp.pad(w_a2, ((0, LANES - GLA_GATE_RANK), (0, 0))).astype(BF16),
        b_a.reshape(1, nqk).astype(F32),
        w_dq.astype(BF16),
        jnp.concatenate([w_dk, w_dv], axis=1).astype(BF16),
        w_iq.astype(BF16),
        _pad_cols(w_ik, LANES).astype(BF16),
        _pad_cols(w_iw, LANES).astype(BF16),
    )
    def lane_freqs(head_dim):
        rot = head_dim // ROPE_FRACTION
        half = rot // 2
        inv = ROPE_THETA ** (-jnp.arange(half, dtype=F32) * 2.0 / rot)
        m = np.arange(LANES) % head_dim
        return jnp.where(jnp.asarray(m < rot), inv[jnp.asarray(m % half)], 0.0)

    invf = jnp.stack([lane_freqs(DSA_HEAD_DIM), lane_freqs(IDX_DIM)]).reshape(2, 1, LANES)
    tabs = _rope_tables(L, invf)

    tq = LANES
    tk = min(L, 512)
    gla_in, g, dq, dkv, dkvt, iq, ik, iw = _in_proj(x, wts, tabs, rows=tk)
    y_gla = _gla(gla_in, g, gla_gain.reshape(1, nv).astype(F32), rows=min(L, 512))
    y_dsa = _dsa(dq, dkv, dkvt, iq, ik, iw, tq, tk)
    return y_gla, y_dsa, w_gate


def _layer(x, w_in, w_a2, b_a, gla_gain, w_branch, w_o, ln1_g, ln1_b, w_up, conv_w, conv_b, w_down, ln2_g, ln2_b,
           depth):
    B, L, D = x.shape
    alpha = (2.0 * depth) ** 0.25
    y_gla, y_dsa, w_gate = _token_mixers(x, w_in, w_a2, b_a, gla_gain)
    x1 = _merge(x, y_gla, y_dsa, w_gate.astype(BF16), w_branch.astype(BF16), w_o.astype(BF16),
                ln1_g.reshape(1, D), ln1_b.reshape(1, D), rows=min(L, 512), alpha=alpha)

    d_ff = w_down.shape[0]
    cw = _ffn_chunk_width(d_ff)
    nch = d_ff // cw
    chunked = lambda a: jnp.concatenate(
        [a[..., :d_ff].reshape(a.shape[:-1] + (nch, cw)), a[..., d_ff:].reshape(a.shape[:-1] + (nch, cw))], axis=-1)
    wup = jnp.moveaxis(chunked(w_up), 1, 0).astype(BF16)
    cwt = jnp.moveaxis(chunked(conv_w), 1, 0)
    cwt = jnp.pad(cwt, ((0, 0), (0, SUBLANES - CONV_WIDTH), (0, 0)))
    cbs = jnp.moveaxis(chunked(conv_b[None, :]), 1, 0)
    return _ffn(x1, wup, cwt, cbs, w_down.astype(BF16), ln2_g.reshape(1, D), ln2_b.reshape(1, D),
                rows=min(L, 512), alpha=alpha)


def kernel(x, w_in, w_gla_a2, b_gla_a, gla_norm_gain, w_branch, w_o, ln1_gain, ln1_bias,
           w_up, conv_w, conv_b, w_down, ln2_gain, ln2_bias):
    depth = w_in.shape[0]
    for layer in range(depth):
        x = _layer(x, w_in[layer], w_gla_a2[layer], b_gla_a[layer], gla_norm_gain[layer], w_branch[layer],
                   w_o[layer], ln1_gain[layer], ln1_bias[layer], w_up[layer], conv_w[layer], conv_b[layer],
                   w_down[layer], ln2_gain[layer], ln2_bias[layer], depth)
    return x
```
